```python
import jax, jax.numpy as jnp
from jax import lax
import numpy as np

D_MODEL = 2048
BATCH = 4
SEQ = 4096
DEPTH = 2

EPS = 1e-6

N_ATT_HEADS = 8
ATT_HEAD_DIM = 128
Q_RANK = 512
KV_RANK = 256
N_IDX_HEADS = 8
IDX_DIM = 64
TOPK_MAX = 256
Q_BLOCK = 128

N_MLSTM_HEADS = 4
MLSTM_QK_DIM = 128
MLSTM_V_DIM = 256
MLSTM_CHUNK = 64
F_BIAS_INIT = 3.0

ATT_WIDTH = N_ATT_HEADS * ATT_HEAD_DIM
MLSTM_WIDTH = N_MLSTM_HEADS * MLSTM_V_DIM
MIX_WIDTH = ATT_WIDTH + MLSTM_WIDTH

IN_SPLITS = (Q_RANK, KV_RANK, IDX_DIM, N_IDX_HEADS,
             N_MLSTM_HEADS * MLSTM_QK_DIM, N_MLSTM_HEADS * MLSTM_QK_DIM,
             MLSTM_WIDTH, MLSTM_WIDTH, N_MLSTM_HEADS, N_MLSTM_HEADS)
D_IN = Q_RANK + KV_RANK + IDX_DIM + N_IDX_HEADS + 2 * N_MLSTM_HEADS * MLSTM_QK_DIM + 2 * MLSTM_WIDTH + 2 * N_MLSTM_HEADS

N_EXPERT_GROUPS = 4
EXPERTS_PER_GROUP = 8
N_EXPERTS = N_EXPERT_GROUPS * EXPERTS_PER_GROUP
TOP_K_EXPERTS = 2
D_FF_EXPERT = 512

kernel_name = "hybrid_dsa_mlstm_grouped_moe_adaln"


def rmsnorm(x, g):
    x32 = x.astype(jnp.float32)
    y = x32 * lax.rsqrt(jnp.mean(x32 * x32, axis=-1, keepdims=True) + EPS)
    return y.astype(x.dtype) * g


def dsa_attention(cq, ckv, k_idx, w_idx, w_uq, w_uk, w_uv, w_qidx):
    B, S, _ = cq.shape
    k_sel = min(TOPK_MAX, S // 4)
    nb = S // Q_BLOCK
    q = (cq @ w_uq).reshape(B, S, N_ATT_HEADS, ATT_HEAD_DIM)
    q_lat = jnp.einsum('bshd,rhd->bshr', q, w_uk)
    q_idx = (cq @ w_qidx).reshape(B, S, N_IDX_HEADS, IDX_DIM)
    w = w_idx * (N_IDX_HEADS ** -0.5 * IDX_DIM ** -0.5)
    scale = ATT_HEAD_DIM ** -0.5
    key_pos = jnp.arange(S)

    def to_blocks(a):
        return a.reshape(B, nb, Q_BLOCK, *a.shape[2:]).swapaxes(0, 1)

    def block(args):
        blk, ql, qi, wb = args
        t = blk * Q_BLOCK + jnp.arange(Q_BLOCK)
        isc = jax.nn.relu(jnp.einsum('bqhd,bsd->bqhs', qi, k_idx).astype(jnp.float32))
        isc = jnp.einsum('bqh,bqhs->bqs', wb.astype(jnp.float32), isc)
        causal = key_pos[None, :] <= t[:, None]
        isc = jnp.where(causal[None], isc, -jnp.inf)
        _, idx = lax.top_k(isc, k_sel)
        kv = jax.vmap(lambda c_b, i_b: c_b[i_b])(ckv, idx)
        s = jnp.einsum('bqhr,bqkr->bqhk', ql, kv).astype(jnp.float32) * scale
        valid = (idx <= t[None, :, None])[:, :, None, :]
        p = jax.nn.softmax(jnp.where(valid, s, -jnp.inf), axis=-1).astype(kv.dtype)
        o_lat = jnp.einsum('bqhk,bqkr->bqhr', p, kv)
        o = jnp.einsum('bqhr,rhd->bqhd', o_lat, w_uv)
        return o.reshape(B, Q_BLOCK, ATT_WIDTH)

    out = lax.map(block, (jnp.arange(nb), to_blocks(q_lat), to_blocks(q_idx), to_blocks(w)))
    return out.swapaxes(0, 1).reshape(B, S, ATT_WIDTH)


def mlstm(q, k, v, ig, fg):
    dtype = v.dtype
    B, S, NH, dk = q.shape
    dv = v.shape[-1]
    L = MLSTM_CHUNK
    nc = S // L

    def chunk4(a):
        return a.astype(jnp.float32).reshape(B, nc, L, NH, a.shape[-1]).transpose(1, 0, 3, 2, 4)

    def chunk3(a):
        return a.astype(jnp.float32).reshape(B, nc, L, NH).transpose(1, 0, 3, 2)

    qc = chunk4(q) * (dk ** -0.5)
    kc, vc = chunk4(k), chunk4(v)
    igc = chunk3(ig)
    lfc = jax.nn.log_sigmoid(chunk3(fg))
    causal = jnp.tril(jnp.ones((L, L), dtype=bool))

    def step(carry, xs):
        C, n, m = carry
        qb, kb, vb, ib, lf = xs
        b = lax.cumsum(lf, axis=2)
        dmat = b[..., :, None] - b[..., None, :] + ib[..., None, :]
        dmat = jnp.where(causal, dmat, -jnp.inf)
        m_inter = b + m[..., None]
        m_j = jnp.maximum(m_inter, dmat.max(-1))
        w_inter = jnp.exp(m_inter - m_j)
        sc = jnp.einsum('bhjd,bhsd->bhjs', qb, kb) * jnp.exp(dmat - m_j[..., None])
        num = w_inter[..., None] * jnp.einsum('bhjd,bhde->bhje', qb, C) + jnp.einsum('bhjs,bhse->bhje', sc, vb)
        den = w_inter * jnp.einsum('bhjd,bhd->bhj', qb, n) + sc.sum(-1)
        h = num / jnp.maximum(jnp.abs(den), jnp.exp(-m_j))[..., None]
        b_last = b[..., -1]
        g = b_last[..., None] - b + ib
        m_new = jnp.maximum(b_last + m, g.max(-1))
        w_old = jnp.exp(b_last + m - m_new)
        w_s = jnp.exp(g - m_new[..., None])
        C_new = w_old[..., None, None] * C + jnp.einsum('bhs,bhsd,bhse->bhde', w_s, kb, vb)
        n_new = w_old[..., None] * n + jnp.einsum('bhs,bhsd->bhd', w_s, kb)
        return (C_new, n_new, m_new), h

    init = (jnp.zeros((B, NH, dk, dv), jnp.float32),
            jnp.zeros((B, NH, dk), jnp.float32),
            jnp.zeros((B, NH), jnp.float32))
    _, h = lax.scan(step, init, (qc, kc, vc, igc, lfc))
    return h.transpose(1, 0, 3, 2, 4).reshape(B, S, NH, dv).astype(dtype)


def grouped_moe(h, router_w, router_b, w_gate_up, w_down):
    B, S, D = h.shape
    T = B * S
    xt = h.reshape(T, D)
    aff = jax.nn.sigmoid((xt @ router_w).astype(jnp.float32))
    sel = aff + router_b.astype(jnp.float32)
    grp = sel.reshape(T, N_EXPERT_GROUPS, EXPERTS_PER_GROUP)
    grp_score = lax.top_k(grp, TOP_K_EXPERTS)[0].sum(-1)
    g_idx = jnp.argmax(grp_score, axis=-1)
    in_grp = jnp.take_along_axis(grp, g_idx[:, None, None], axis=1)[:, 0]
    _, loc = lax.top_k(in_grp, TOP_K_EXPERTS)
    e_idx = g_idx[:, None] * EXPERTS_PER_GROUP + loc
    a = jnp.take_along_axis(aff, e_idx, axis=1)
    gates = a / a.sum(-1, keepdims=True)
    e_flat = e_idx.reshape(-1)
    tok = jnp.repeat(jnp.arange(T), TOP_K_EXPERTS)
    order = jnp.argsort(e_flat)
    tok_s = tok[order]
    gate_s = gates.reshape(-1)[order]
    group_sizes = jnp.bincount(e_flat, length=N_EXPERTS).astype(jnp.int32)
    xs = xt[tok_s]
    gu = lax.ragged_dot(xs, w_gate_up, group_sizes)
    gt, up = jnp.split(gu, 2, axis=-1)
    ys = lax.ragged_dot(jax.nn.silu(gt) * up, w_down, group_sizes) * gate_s[:, None].astype(xt.dtype)
    out = jnp.zeros_like(xt).at[tok_s].add(ys)
    return out.reshape(B, S, D)


def setup_inputs(seed: int = 0) -> dict:
    key = jax.random.key(seed)
    ks = jax.random.split(key, 24)

    def nrm(k, shape, scale):
        return jax.random.normal(k, shape, jnp.float32) * scale

    def gain(k, shape):
        return 1.0 + 0.01 * jax.random.normal(k, shape, jnp.float32)

    return {
        "x": nrm(ks[0], (BATCH, SEQ, D_MODEL), 1.0),
        "c": nrm(ks[1], (BATCH, D_MODEL), 1.0),
        "ada_w": nrm(ks[2], (DEPTH, D_MODEL, 6 * D_MODEL), 0.5 * D_MODEL ** -0.5),
        "ada_b": nrm(ks[3], (DEPTH, 6 * D_MODEL), 0.01),
        "mix_norm_g": gain(ks[4], (DEPTH, D_MODEL)),
        "w_in": nrm(ks[5], (DEPTH, D_MODEL, D_IN), D_MODEL ** -0.5),
        "cq_norm_g": gain(ks[6], (DEPTH, Q_RANK)),
        "ckv_norm_g": gain(ks[7], (DEPTH, KV_RANK)),
        "w_uq": nrm(ks[8], (DEPTH, Q_RANK, ATT_WIDTH), Q_RANK ** -0.5),
        "w_uk": nrm(ks[9], (DEPTH, KV_RANK, N_ATT_HEADS, ATT_HEAD_DIM), KV_RANK ** -0.5),
        "w_uv": nrm(ks[10], (DEPTH, KV_RANK, N_ATT_HEADS, ATT_HEAD_DIM), KV_RANK ** -0.5),
        "w_qidx": nrm(ks[11], (DEPTH, Q_RANK, N_IDX_HEADS * IDX_DIM), Q_RANK ** -0.5),
        "mlstm_i_b": nrm(ks[12], (DEPTH, N_MLSTM_HEADS), 0.1),
        "mlstm_f_b": F_BIAS_INIT + 3.0 * jax.random.uniform(ks[13], (DEPTH, N_MLSTM_HEADS), jnp.float32),
        "mlstm_norm_g": gain(ks[14], (DEPTH, N_MLSTM_HEADS, MLSTM_V_DIM)),
        "w_out": nrm(ks[15], (DEPTH, MIX_WIDTH, D_MODEL), MIX_WIDTH ** -0.5),
        "ffn_norm_g": gain(ks[16], (DEPTH, D_MODEL)),
        "router_w": nrm(ks[17], (D_MODEL, N_EXPERTS), D_MODEL ** -0.5),
        "router_b": nrm(ks[18], (N_EXPERTS,), 0.01),
        "w_gate_up": nrm(ks[19], (DEPTH, N_EXPERTS, D_MODEL, 2 * D_FF_EXPERT), D_MODEL ** -0.5),
        "w_down": nrm(ks[20], (DEPTH, N_EXPERTS, D_FF_EXPERT, D_MODEL), D_FF_EXPERT ** -0.5),
        "final_norm_g": gain(ks[21], (D_MODEL,)),
    }


def reference(x, c, ada_w, ada_b, mix_norm_g, w_in, cq_norm_g, ckv_norm_g, w_uq, w_uk, w_uv,
              w_qidx, mlstm_i_b, mlstm_f_b, mlstm_norm_g, w_out, ffn_norm_g, router_w, router_b,
              w_gate_up, w_down, final_norm_g):
    B, S, _ = x.shape
    NH = N_MLSTM_HEADS
    split_pts = np.cumsum(IN_SPLITS)[:-1].tolist()
    c_act = jax.nn.silu(c)
    for l in range(DEPTH):
        mod = (c_act @ ada_w[l] + ada_b[l])[:, None, :]
        sh1, sc1, g1, sh2, sc2, g2 = jnp.split(mod, 6, axis=-1)
        h = rmsnorm(x, mix_norm_g[l]) * (1 + sc1) + sh1
        z = h @ w_in[l]
        cq, ckv, k_idx, w_idx, mq, mk, mv, mo, mi, mf = jnp.split(z, split_pts, axis=-1)
        att = dsa_attention(rmsnorm(cq, cq_norm_g[l]), rmsnorm(ckv, ckv_norm_g[l]), k_idx, w_idx,
                            w_uq[l], w_uk[l], w_uv[l], w_qidx[l])
        hm = mlstm(mq.reshape(B, S, NH, MLSTM_QK_DIM), mk.reshape(B, S, NH, MLSTM_QK_DIM),
                   mv.reshape(B, S, NH, MLSTM_V_DIM), mi + mlstm_i_b[l], mf + mlstm_f_b[l])
        hm = rmsnorm(hm, mlstm_norm_g[l]).reshape(B, S, MLSTM_WIDTH) * jax.nn.sigmoid(mo)
        y = jnp.concatenate([att, hm], axis=-1) @ w_out[l]
        x = x + g1 * y
        h2 = rmsnorm(x, ffn_norm_g[l]) * (1 + sc2) + sh2
        x = x + g2 * grouped_moe(h2, router_w, router_b, w_gate_up[l], w_down[l])
    return rmsnorm(x, final_norm_g)
```

```python
import functools

import jax
import jax.numpy as jnp
from jax import lax
from jax.experimental import pallas as pl
from jax.experimental.pallas import tpu as pltpu

F32 = jnp.float32
BF16 = jnp.bfloat16
I32 = jnp.int32

EPS = 1e-6

N_ATT_HEADS = 8
ATT_HEAD_DIM = 128
Q_RANK = 512
KV_RANK = 256
N_IDX_HEADS = 8
IDX_DIM = 64
TOPK_MAX = 256
Q_BLOCK = 128
N_MLSTM_HEADS = 4
MLSTM_QK_DIM = 128
MLSTM_V_DIM = 256
N_EXPERT_GROUPS = 4
EXPERTS_PER_GROUP = 8
N_EXPERTS = N_EXPERT_GROUPS * EXPERTS_PER_GROUP
D_FF_EXPERT = 512
ATT_WIDTH = N_ATT_HEADS * ATT_HEAD_DIM
MLSTM_WIDTH = N_MLSTM_HEADS * MLSTM_V_DIM

LANES = 128
SUBLANES = 8
VMEM_LIMIT = 52 * 1024 * 1024

Z_CQ = 0
Z_CKV = 512
Z_KW = 768
Z_MQ = 1024
Z_MK = 1536
Z_MV = 2048
Z_MO = 3072
Z_WIDTH = 4096
KW_WIDX = 64
KW_MI = 72
KW_MF = 76

NEG_BIG = -1e30
INT_MIN = -2147483648


def _cparams(sem):
    return pltpu.CompilerParams(dimension_semantics=sem, vmem_limit_bytes=VMEM_LIMIT)


def _rms(x):
    return x * lax.rsqrt(jnp.mean(x * x, axis=-1, keepdims=True) + EPS)


def _dot(a, b):
    return jnp.dot(a, b, preferred_element_type=F32)


def _dot_nt(a, b):
    return lax.dot_general(a, b, (((1,), (1,)), ((), ())), preferred_element_type=F32)


def _dot_f32(a, b):
    return jnp.dot(a, b, preferred_element_type=F32, precision=lax.Precision.HIGHEST)


def _mod_kernel(c_ref, w_ref, b_ref, o_ref):
    c = c_ref[...]
    ca = (c * jax.nn.sigmoid(c)).astype(BF16)
    o_ref[...] = _dot(ca, w_ref[...].astype(BF16)) + b_ref[...]


def _modulation(c, ada_w, ada_b):
    depth, d, n = ada_w.shape
    bsz = c.shape[0]
    rows = ((bsz + SUBLANES - 1) // SUBLANES) * SUBLANES
    c_pad = jnp.pad(c, ((0, rows - bsz), (0, 0)))
    tn = 1024
    out = pl.pallas_call(
        _mod_kernel,
        out_shape=jax.ShapeDtypeStruct((depth, rows, n), F32),
        grid=(depth, n // tn),
        in_specs=[
            pl.BlockSpec((rows, d), lambda l, j: (0, 0)),
            pl.BlockSpec((None, d, tn), lambda l, j: (l, 0, j)),
            pl.BlockSpec((None, 1, tn), lambda l, j: (l, 0, j)),
        ],
        out_specs=pl.BlockSpec((None, rows, tn), lambda l, j: (l, 0, j)),
        compiler_params=_cparams(("arbitrary", "arbitrary")),
        name="ada_modulation",
    )(c_pad, ada_w, ada_b.reshape(depth, 1, n))
    return out[:, :bsz]


def _inproj_kernel(x_ref, g_ref, sc_ref, sh_ref, w_ref, z_ref, h_ref):
    @pl.when(pl.program_id(1) == 0)
    def _():
        y = _rms(x_ref[...]) * g_ref[...]
        h_ref[...] = (y * (1.0 + sc_ref[...]) + sh_ref[...]).astype(BF16)

    z_ref[...] = _dot(h_ref[...], w_ref[...])


def _in_projection(x2d, gain, mod_l, w_in_r, seq):
    t, d = x2d.shape
    tm = min(512, seq)
    tn = 1024
    per_b = seq // tm
    return pl.pallas_call(
        _inproj_kernel,
        out_shape=jax.ShapeDtypeStruct((t, Z_WIDTH), F32),
        grid=(t // tm, Z_WIDTH // tn),
        in_specs=[
            pl.BlockSpec((tm, d), lambda i, j: (i, 0)),
            pl.BlockSpec((1, d), lambda i, j: (0, 0)),
            pl.BlockSpec((None, None, 1, d), lambda i, j: (i // per_b, 1, 0, 0)),
            pl.BlockSpec((None, None, 1, d), lambda i, j: (i // per_b, 0, 0, 0)),
            pl.BlockSpec((d, tn), lambda i, j: (0, j)),
        ],
        out_specs=pl.BlockSpec((tm, tn), lambda i, j: (i, j)),
        scratch_shapes=[pltpu.VMEM((tm, d), BF16)],
        compiler_params=_cparams(("arbitrary", "arbitrary")),
        name="norm_in_projection",
    )(x2d, gain.reshape(1, d), mod_l, mod_l, w_in_r)


def _dsa_prep_kernel(cq_ref, ckv_ref, kw_ref, gq_ref, gkv_ref, wuq_ref, wuk_ref, wqi_ref,
                     qlat_ref, qidx_ref, ckvn_ref, kidx_ref):
    cqn = (_rms(cq_ref[...]) * gq_ref[...]).astype(BF16)
    ckvn_ref[...] = (_rms(ckv_ref[...]) * gkv_ref[...]).astype(BF16)
    q = _dot(cqn, wuq_ref[...]).astype(BF16)
    qi = _dot(cqn, wqi_ref[...])
    scale = ATT_HEAD_DIM ** -0.5
    for h in range(N_ATT_HEADS):
        ql = _dot(q[:, h * ATT_HEAD_DIM:(h + 1) * ATT_HEAD_DIM], wuk_ref[h]) * scale
        qlat_ref[h] = ql.astype(BF16)
    for h in range(N_IDX_HEADS):
        qidx_ref[h] = qi[:, h * LANES:(h + 1) * LANES].astype(BF16)
    kw = kw_ref[...]
    lane = lax.broadcasted_iota(I32, kw.shape, 1)
    kidx_ref[...] = jnp.where(lane < IDX_DIM, kw, 0.0).astype(BF16)


def _dsa_prep(z, gq, gkv, w_uq, w_ukt, w_qidx_p, bsz, seq):
    t = z.shape[0]
    tm = min(256, seq)
    per_b = seq // tm
    return pl.pallas_call(
        _dsa_prep_kernel,
        out_shape=(
            jax.ShapeDtypeStruct((bsz, N_ATT_HEADS, seq, KV_RANK), BF16),
            jax.ShapeDtypeStruct((bsz, N_IDX_HEADS, seq, LANES), BF16),
            jax.ShapeDtypeStruct((t, KV_RANK), BF16),
            jax.ShapeDtypeStruct((t, LANES), BF16),
        ),
        grid=(t // tm,),
        in_specs=[
            pl.BlockSpec((tm, Q_RANK), lambda i: (i, Z_CQ // Q_RANK)),
            pl.BlockSpec((tm, KV_RANK), lambda i: (i, Z_CKV // KV_RANK)),
            pl.BlockSpec((tm, LANES), lambda i: (i, Z_KW // LANES)),
            pl.BlockSpec((1, Q_RANK), lambda i: (0, 0)),
            pl.BlockSpec((1, KV_RANK), lambda i: (0, 0)),
            pl.BlockSpec((Q_RANK, ATT_WIDTH), lambda i: (0, 0)),
            pl.BlockSpec((N_ATT_HEADS, ATT_HEAD_DIM, KV_RANK), lambda i: (0, 0, 0)),
            pl.BlockSpec((Q_RANK, N_IDX_HEADS * LANES), lambda i: (0, 0)),
        ],
        out_specs=(
            pl.BlockSpec((None, N_ATT_HEADS, tm, KV_RANK), lambda i: (i // per_b, 0, i % per_b, 0)),
            pl.BlockSpec((None, N_IDX_HEADS, tm, LANES), lambda i: (i // per_b, 0, i % per_b, 0)),
            pl.BlockSpec((tm, KV_RANK), lambda i: (i, 0)),
            pl.BlockSpec((tm, LANES), lambda i: (i, 0)),
        ),
        compiler_params=_cparams(("arbitrary",)),
        name="dsa_prep",
    )(z, z, z, gq, gkv, w_uq, w_ukt, w_qidx_p)


def _dsa_attn_kernel(qidx_ref, qlat_ref, kw_ref, kidx_ref, ckvn_ref, wuv_ref, o_ref,
                     key_ref, m_ref, l_ref, a_ref, acc_ref, p_ref, lo_ref, jl_ref,
                     *, kc, k_sel, seq):
    qb = Q_BLOCK
    i = pl.program_id(1)
    nkeys = (i + 1) * qb
    nch = (nkeys + kc - 1) // kc
    t_col = i * qb + lax.broadcasted_iota(I32, (qb, 1), 0)
    lane_pos = lax.broadcasted_iota(I32, (qb, kc), 1)
    w_all = kw_ref[...] * (N_IDX_HEADS ** -0.5 * IDX_DIM ** -0.5)

    def idx_body(c, carry):
        ks = kidx_ref[pl.ds(pl.multiple_of(c * kc, kc), kc), :]
        acc = jnp.zeros((qb, kc), F32)
        for h in range(N_IDX_HEADS):
            r = _dot_nt(qidx_ref[h], ks)
            acc = acc + w_all[:, KW_WIDX + h:KW_WIDX + h + 1] * jnp.maximum(r, 0.0)
        pos = c * kc + lane_pos
        acc = jnp.where(pos <= t_col, acc + 0.0, -jnp.inf)
        bits = pltpu.bitcast(acc, I32)
        key_ref[c] = bits ^ ((bits >> 31) & 0x7FFFFFFF)
        return carry

    lax.fori_loop(0, nch, idx_body, 0)

    def count(pred_fn):
        def body(c, acc):
            ind = jnp.where(pred_fn(key_ref[c], c * kc + lane_pos), 1.0, 0.0)
            for g in range(kc // LANES):
                acc = acc + ind[:, g * LANES:(g + 1) * LANES]
            return acc
        acc = lax.fori_loop(0, nch, body, jnp.zeros((qb, LANES), F32))
        return jnp.sum(acc, axis=1, keepdims=True)

    kf = float(k_sel)

    def bit_body(it, lo):
        step = lax.shift_left(jnp.int32(1), jnp.int32(31) - it)
        cand = lo + step
        cnt = count(lambda k, pos: k >= cand)
        return jnp.where(cnt >= kf, cand, lo)

    lo = lax.fori_loop(0, 32, bit_body, jnp.full((qb, 1), INT_MIN, I32))
    lo_ref[...] = lo

    cnt_gt = count(lambda k, pos: k > lo)
    cnt_eq = count(lambda k, pos: k == lo)
    need = kf - cnt_gt
    neg_inf_key = INT_MIN + 0x7FFFFF
    bad = jnp.where((cnt_eq > need) & (lo > neg_inf_key), 1.0, 0.0)
    jl_ref[...] = jnp.full((qb, 1), seq, I32)

    @pl.when(jnp.max(bad) > 0.0)
    def _():
        nbits = max(1, (seq - 1).bit_length())

        def jbit_body(it, v):
            cand = v + lax.shift_left(jnp.int32(1), jnp.int32(nbits - 1) - it)
            cnt = count(lambda k, pos: (k == lo) & (pos < cand))
            return jnp.where(cnt < need, cand, v)

        jl_ref[...] = lax.fori_loop(0, nbits, jbit_body, jnp.zeros((qb, 1), I32))

    m_ref[...] = jnp.full(m_ref.shape, NEG_BIG, F32)
    l_ref[...] = jnp.zeros(l_ref.shape, F32)
    acc_ref[...] = jnp.zeros(acc_ref.shape, F32)

    def att_body(c, carry):
        kv = ckvn_ref[pl.ds(pl.multiple_of(c * kc, kc), kc), :]
        key = key_ref[c]
        pos = c * kc + lane_pos
        lo_c = lo_ref[...]
        tie = jnp.where(key == lo_c, jnp.where(pos <= jl_ref[...], 0.0, NEG_BIG), NEG_BIG)
        bias = jnp.where(pos <= t_col, jnp.where(key > lo_c, 0.0, tie), NEG_BIG)
        for h in range(N_ATT_HEADS):
            s = _dot_nt(qlat_ref[h], kv) + bias
            m_old = m_ref[h]
            m_new = jnp.maximum(m_old, jnp.max(s, axis=1, keepdims=True))
            alpha = jnp.exp(m_old - m_new)
            p = jnp.exp(s - m_new)
            l_ref[h] = alpha * l_ref[h] + jnp.sum(p, axis=1, keepdims=True)
            m_ref[h] = m_new
            a_ref[h] = alpha
            p_ref[h * qb:(h + 1) * qb, :] = p.astype(BF16)
        pv = _dot(p_ref[...], kv)
        for h in range(N_ATT_HEADS):
            acc_ref[h] = acc_ref[h] * a_ref[h] + pv[h * qb:(h + 1) * qb]
        return carry

    lax.fori_loop(0, nch, att_body, 0)

    for h in range(N_ATT_HEADS):
        o_lat = (acc_ref[h] / l_ref[h]).astype(BF16)
        o_ref[:, h * ATT_HEAD_DIM:(h + 1) * ATT_HEAD_DIM] = _dot(o_lat, wuv_ref[h]).astype(o_ref.dtype)


def _dsa_attention(qidx, qlat, z, kidx, ckvn, w_uvh, bsz, seq):
    t = z.shape[0]
    qb = Q_BLOCK
    kc = min(512, seq)
    nb = seq // qb
    k_sel = min(TOPK_MAX, seq // 4)
    kern = functools.partial(_dsa_attn_kernel, kc=kc, k_sel=k_sel, seq=seq)
    return pl.pallas_call(
        kern,
        out_shape=jax.ShapeDtypeStruct((t, ATT_WIDTH), BF16),
        grid=(bsz, nb),
        in_specs=[
            pl.BlockSpec((None, N_IDX_HEADS, qb, LANES), lambda b, i: (b, 0, i, 0)),
            pl.BlockSpec((None, N_ATT_HEADS, qb, KV_RANK), lambda b, i: (b, 0, i, 0)),
            pl.BlockSpec((qb, LANES), lambda b, i: (b * nb + i, Z_KW // LANES)),
            pl.BlockSpec((seq, LANES), lambda b, i: (b, 0)),
            pl.BlockSpec((seq, KV_RANK), lambda b, i: (b, 0)),
            pl.BlockSpec((N_ATT_HEADS, KV_RANK, ATT_HEAD_DIM), lambda b, i: (0, 0, 0)),
        ],
        out_specs=pl.BlockSpec((qb, ATT_WIDTH), lambda b, i: (b * nb + i, 0)),
        scratch_shapes=[
            pltpu.VMEM((seq // kc, qb, kc), I32),
            pltpu.VMEM((N_ATT_HEADS, qb, 1), F32),
            pltpu.VMEM((N_ATT_HEADS, qb, 1), F32),
            pltpu.VMEM((N_ATT_HEADS, qb, 1), F32),
            pltpu.VMEM((N_ATT_HEADS, qb, KV_RANK), F32),
            pltpu.VMEM((N_ATT_HEADS * qb, kc), BF16),
            pltpu.VMEM((qb, 1), I32),
            pltpu.VMEM((qb, 1), I32),
        ],
        compiler_params=_cparams(("arbitrary", "arbitrary")),
        name="dsa_attention",
    )(qidx, qlat, z, kidx, ckvn, w_uvh)


def _log_sigmoid(x):
    return jnp.minimum(x, 0.0) - jnp.log(1.0 + jnp.exp(-jnp.abs(x)))


def _mlstm_kernel(q_ref, k_ref, v_ref, og_ref, kw_ref, gb_ref, gn_ref, out_ref,
                  c_ref, n_ref, m_ref, *, chunk):
    lc = chunk
    dk, dv = MLSTM_QK_DIM, MLSTM_V_DIM

    @pl.when(pl.program_id(1) == 0)
    def _():
        c_ref[...] = jnp.zeros(c_ref.shape, F32)
        n_ref[...] = jnp.zeros(n_ref.shape, F32)
        m_ref[...] = jnp.zeros(m_ref.shape, F32)

    kwb = kw_ref[...] + gb_ref[...]
    kwt = jnp.transpose(kwb)
    row = lax.broadcasted_iota(I32, (lc, lc), 0)
    col = lax.broadcasted_iota(I32, (lc, lc), 1)
    causal = col <= row
    tri = jnp.where(causal, 1.0, 0.0)
    tri_t = jnp.where(row <= col, 1.0, 0.0)
    cum_col = _dot_f32(tri, _log_sigmoid(kwb))
    cum_row = _dot_f32(_log_sigmoid(kwt), tri_t)

    for h in range(N_MLSTM_HEADS):
        i_col = kwb[:, KW_MI + h:KW_MI + h + 1]
        i_row = kwt[KW_MI + h:KW_MI + h + 1, :]
        b_col = cum_col[:, KW_MF + h:KW_MF + h + 1]
        b_row = cum_row[KW_MF + h:KW_MF + h + 1, :]
        m_prev = m_ref[h]

        dmat = jnp.where(causal, b_col - b_row + i_row, -jnp.inf)
        m_inter = b_col + m_prev
        m_j = jnp.maximum(m_inter, jnp.max(dmat, axis=1, keepdims=True))
        w_inter = jnp.exp(m_inter - m_j)

        qh = q_ref[:, h * dk:(h + 1) * dk] * (dk ** -0.5)
        kh = k_ref[:, h * dk:(h + 1) * dk]
        vb = v_ref[:, h * dv:(h + 1) * dv].astype(BF16)
        qb = qh.astype(BF16)
        sc = _dot_nt(qb, kh.astype(BF16)) * jnp.exp(dmat - m_j)
        c_old = c_ref[h]
        n_old = n_ref[h]
        num = w_inter * _dot(qb, c_old.astype(BF16)) + _dot(sc.astype(BF16), vb)
        den = (w_inter * jnp.sum(qh * n_old, axis=1, keepdims=True)
               + jnp.sum(sc, axis=1, keepdims=True))
        hval = num / jnp.maximum(jnp.abs(den), jnp.exp(-m_j))

        hn = _rms(hval) * gn_ref[:, h * dv:(h + 1) * dv]
        gate = jax.nn.sigmoid(og_ref[:, h * dv:(h + 1) * dv])
        out_ref[:, h * dv:(h + 1) * dv] = (hn * gate).astype(out_ref.dtype)

        b_last = b_col[lc - 1:lc, :]
        g_col = b_last - b_col + i_col
        m_new = jnp.maximum(b_last + m_prev, jnp.max(g_col, axis=0, keepdims=True))
        w_old = jnp.exp(b_last + m_prev - m_new)
        kwgt = kh * jnp.exp(g_col - m_new)
        c_ref[h] = w_old * c_old + _dot(jnp.transpose(kwgt).astype(BF16), vb)
        n_ref[h] = w_old * n_old + jnp.sum(kwgt, axis=0, keepdims=True)
        m_ref[h] = m_new


def _mlstm(z, gate_bias, norm_g, bsz, seq):
    t = z.shape[0]
    lc = min(256, seq)
    nc = seq // lc
    qk_w = N_MLSTM_HEADS * MLSTM_QK_DIM
    kern = functools.partial(_mlstm_kernel, chunk=lc)
    row = lambda b, c: b * nc + c
    return pl.pallas_call(
        kern,
        out_shape=jax.ShapeDtypeStruct((t, MLSTM_WIDTH), BF16),
        grid=(bsz, nc),
        in_specs=[
            pl.BlockSpec((lc, qk_w), lambda b, c: (row(b, c), Z_MQ // qk_w)),
            pl.BlockSpec((lc, qk_w), lambda b, c: (row(b, c), Z_MK // qk_w)),
            pl.BlockSpec((lc, MLSTM_WIDTH), lambda b, c: (row(b, c), Z_MV // MLSTM_WIDTH)),
            pl.BlockSpec((lc, MLSTM_WIDTH), lambda b, c: (row(b, c), Z_MO // MLSTM_WIDTH)),
            pl.BlockSpec((lc, LANES), lambda b, c: (row(b, c), Z_KW // LANES)),
            pl.BlockSpec((1, LANES), lambda b, c: (0, 0)),
            pl.BlockSpec((1, MLSTM_WIDTH), lambda b, c: (0, 0)),
        ],
        out_specs=pl.BlockSpec((lc, MLSTM_WIDTH), lambda b, c: (row(b, c), 0)),
        scratch_shapes=[
            pltpu.VMEM((N_MLSTM_HEADS, MLSTM_QK_DIM, MLSTM_V_DIM), F32),
            pltpu.VMEM((N_MLSTM_HEADS, 1, MLSTM_QK_DIM), F32),
            pltpu.VMEM((N_MLSTM_HEADS, 1, 1), F32),
        ],
        compiler_params=_cparams(("arbitrary", "arbitrary")),
        name="mlstm",
    )(z, z, z, z, z, gate_bias, norm_g)


def _outproj_router_kernel(att_ref, hm_ref, x_ref, g1_ref, gn_ref, sc_ref, sh_ref, wo_ref, rw_ref, rb_ref,
                           x1_ref, h2_ref, ri_ref, rf_ref, cnt_ref, carry_ref, *, tm):
    @pl.when(pl.program_id(0) == 0)
    def _():
        carry_ref[...] = jnp.zeros(carry_ref.shape, F32)

    y = _dot(att_ref[...], wo_ref[:ATT_WIDTH, :]) + _dot(hm_ref[...], wo_ref[ATT_WIDTH:, :])
    x1 = x_ref[...] + g1_ref[...] * y
    x1_ref[...] = x1
    h2 = _rms(x1) * gn_ref[...] * (1.0 + sc_ref[...]) + sh_ref[...]
    h2_ref[...] = h2

    logits = _dot_f32(h2, rw_ref[...])
    aff = jax.nn.sigmoid(jnp.transpose(logits)[:N_EXPERTS, :])
    sel = aff + rb_ref[...]

    epg = EXPERTS_PER_GROUP
    riota = lax.broadcasted_iota(I32, (epg, tm), 0)
    best = None
    for g in range(N_EXPERT_GROUPS):
        v = sel[g * epg:(g + 1) * epg]
        a = aff[g * epg:(g + 1) * epg]
        m1 = jnp.max(v, axis=0, keepdims=True)
        i1 = jnp.min(jnp.where(v == m1, riota, epg), axis=0, keepdims=True)
        v2 = jnp.where(riota == i1, -jnp.inf, v)
        m2 = jnp.max(v2, axis=0, keepdims=True)
        i2 = jnp.min(jnp.where(v2 == m2, riota, epg), axis=0, keepdims=True)
        a1 = jnp.sum(jnp.where(riota == i1, a, 0.0), axis=0, keepdims=True)
        a2 = jnp.sum(jnp.where(riota == i2, a, 0.0), axis=0, keepdims=True)
        cur = (m1 + m2, i1 + g * epg, i2 + g * epg, a1, a2)
        if best is None:
            best = cur
        else:
            better = cur[0] > best[0]
            best = tuple(jnp.where(better, c_, b_) for c_, b_ in zip(cur, best))
    _, e1, e2, a1, a2 = best
    asum = a1 + a2

    eiota = lax.broadcasted_iota(I32, (N_EXPERTS, tm), 0)
    hit1 = eiota == e1
    hit2 = eiota == e2
    onehot = jnp.where(hit1, 1.0, jnp.where(hit2, 1.0, 0.0))
    srow = lax.broadcasted_iota(I32, (tm, tm), 0)
    scol = lax.broadcasted_iota(I32, (tm, tm), 1)
    before = jnp.where(srow < scol, 1.0, 0.0).astype(BF16)
    rank = _dot(onehot.astype(BF16), before) + carry_ref[...]
    r1 = jnp.sum(jnp.where(hit1, rank, 0.0), axis=0, keepdims=True)
    r2 = jnp.sum(jnp.where(hit2, rank, 0.0), axis=0, keepdims=True)
    carry = carry_ref[...] + jnp.sum(onehot, axis=1, keepdims=True)
    carry_ref[...] = carry
    cnt_ref[...] = jnp.broadcast_to(carry, cnt_ref.shape)

    zi = jnp.zeros((1, tm), I32)
    ri_ref[...] = jnp.concatenate(
        [e1, e2, r1.astype(I32), r2.astype(I32), zi, zi, zi, zi], axis=0)
    zf = jnp.zeros((1, tm), F32)
    rf_ref[...] = jnp.concatenate([a1 / asum, a2 / asum, zf, zf, zf, zf, zf, zf], axis=0)


def _outproj_router(att, hm, x2d, mod_l, gain, w_out_b, router_wp, router_b, seq):
    t, d = x2d.shape
    tm = min(256, seq)
    per_b = seq // tm
    nt = t // tm
    kern = functools.partial(_outproj_router_kernel, tm=tm)
    mod_spec = lambda which: pl.BlockSpec((None, None, 1, d), lambda i: (i // per_b, which, 0, 0))
    return pl.pallas_call(
        kern,
        out_shape=(
            jax.ShapeDtypeStruct((t, d), F32),
            jax.ShapeDtypeStruct((t, d), F32),
            jax.ShapeDtypeStruct((nt, SUBLANES, tm), I32),
            jax.ShapeDtypeStruct((nt, SUBLANES, tm), F32),
            jax.ShapeDtypeStruct((N_EXPERTS, LANES), F32),
        ),
        grid=(nt,),
        in_specs=[
            pl.BlockSpec((tm, ATT_WIDTH), lambda i: (i, 0)),
            pl.BlockSpec((tm, MLSTM_WIDTH), lambda i: (i, 0)),
            pl.BlockSpec((tm, d), lambda i: (i, 0)),
            mod_spec(2),
            pl.BlockSpec((1, d), lambda i: (0, 0)),
            mod_spec(4),
            mod_spec(3),
            pl.BlockSpec((ATT_WIDTH + MLSTM_WIDTH, d), lambda i: (0, 0)),
            pl.BlockSpec((d, LANES), lambda i: (0, 0)),
            pl.BlockSpec((N_EXPERTS, 1), lambda i: (0, 0)),
        ],
        out_specs=(
            pl.BlockSpec((tm, d), lambda i: (i, 0)),
            pl.BlockSpec((tm, d), lambda i: (i, 0)),
            pl.BlockSpec((None, SUBLANES, tm), lambda i: (i, 0, 0)),
            pl.BlockSpec((None, SUBLANES, tm), lambda i: (i, 0, 0)),
            pl.BlockSpec((N_EXPERTS, LANES), lambda i: (0, 0)),
        ),
        scratch_shapes=[pltpu.VMEM((N_EXPERTS, 1), F32)],
        compiler_params=_cparams(("arbitrary",)),
        name="outproj_router",
    )(att, hm, x2d, mod_l, gain.reshape(1, d), mod_l, mod_l, w_out_b, router_wp, router_b.reshape(N_EXPERTS, 1))


def _dispatch_kernel(pos_ref, src_ref, dst_ref, sem, *, n_tok, group):
    def row_copy(tok, slot):
        p = pos_ref[2 * tok + slot]
        return pltpu.make_async_copy(src_ref.at[pl.ds(tok, 1)], dst_ref.at[pl.ds(p, 1)], sem)

    def start_group(g):
        for r in range(group):
            for slot in range(2):
                row_copy(g * group + r, slot).start()

    def wait_group(g):
        for r in range(group):
            for slot in range(2):
                row_copy(g * group + r, slot).wait()

    n_groups = n_tok // group
    start_group(0)

    def body(g, carry):
        start_group(g)
        wait_group(g - 1)
        return carry

    lax.fori_loop(1, n_groups, body, 0)
    wait_group(n_groups - 1)


def _dispatch(pos, h2):
    t, d = h2.shape
    kern = functools.partial(_dispatch_kernel, n_tok=t, group=16)
    return pl.pallas_call(
        kern,
        out_shape=jax.ShapeDtypeStruct((2 * t, d), h2.dtype),
        grid_spec=pltpu.PrefetchScalarGridSpec(
            num_scalar_prefetch=1,
            grid=(1,),
            in_specs=[pl.BlockSpec(memory_space=pl.ANY)],
            out_specs=pl.BlockSpec(memory_space=pl.ANY),
            scratch_shapes=[pltpu.SemaphoreType.DMA],
        ),
        compiler_params=pltpu.CompilerParams(dimension_semantics=("arbitrary",), has_side_effects=True),
        name="moe_dispatch",
    )(pos, h2)


def _gmm_kernel(vt_ref, ve_ref, vfirst_ref, vstart_ref, vend_ref, xs_ref, wgu_ref, wd_ref, ys_ref, *, tm):
    v = pl.program_id(0)
    start = vstart_ref[v]
    end = vend_ref[v]

    @pl.when(end > start)
    def _():
        x = xs_ref[...].astype(BF16)
        gu = _dot(x, wgu_ref[...])
        gt = gu[:, :D_FF_EXPERT]
        up = gu[:, D_FF_EXPERT:]
        act = (gt * jax.nn.sigmoid(gt)) * up
        y = _dot(act.astype(BF16), wd_ref[...])
        rows = vt_ref[v] * tm + lax.broadcasted_iota(I32, (tm, 1), 0)
        mine = (rows >= start) & (rows < end)

        @pl.when(vfirst_ref[v] == 1)
        def _():
            ys_ref[...] = jnp.where(mine, y, 0.0)

        @pl.when(vfirst_ref[v] == 0)
        def _():
            ys_ref[...] = jnp.where(mine, y, ys_ref[...])


def _gmm(sched, xs, wgu_b, wd_b, tm):
    rows, d = xs.shape
    n_visits = sched[0].shape[0]
    kern = functools.partial(_gmm_kernel, tm=tm)
    return pl.pallas_call(
        kern,
        out_shape=jax.ShapeDtypeStruct((rows, d), F32),
        grid_spec=pltpu.PrefetchScalarGridSpec(
            num_scalar_prefetch=5,
            grid=(n_visits,),
            in_specs=[
                pl.BlockSpec((tm, d), lambda v, vt, ve, vf, vs, vn: (vt[v], 0)),
                pl.BlockSpec((None, d, 2 * D_FF_EXPERT), lambda v, vt, ve, vf, vs, vn: (ve[v], 0, 0)),
                pl.BlockSpec((None, D_FF_EXPERT, d), lambda v, vt, ve, vf, vs, vn: (ve[v], 0, 0)),
            ],
            out_specs=pl.BlockSpec((tm, d), lambda v, vt, ve, vf, vs, vn: (vt[v], 0)),
        ),
        compiler_params=_cparams(("arbitrary",)),
        name="moe_experts",
    )(*sched, xs, wgu_b, wd_b)


def _visit_schedule(counts, n_rows, tm):
    ends = jnp.cumsum(counts)
    starts = ends - counts
    n_tiles = n_rows // tm
    n_visits = n_tiles + N_EXPERTS - 1
    first_tile = starts // tm
    last_tile = jnp.maximum(ends - 1, 0) // tm
    nv = jnp.where(counts > 0, last_tile - first_tile + 1, 0)
    vend_cum = jnp.cumsum(nv)
    voff = vend_cum - nv
    total = vend_cum[-1]
    v = jnp.arange(n_visits, dtype=I32)
    live = v < total
    e = jnp.searchsorted(vend_cum, jnp.minimum(v, total - 1), side="right").astype(I32)
    e = jnp.minimum(e, N_EXPERTS - 1)
    tile = jnp.where(live, first_tile[e] + (v - voff[e]), n_tiles - 1).astype(I32)
    prev_tile = jnp.concatenate([jnp.full((1,), -1, I32), tile[:-1]])
    first = jnp.where(live & (tile != prev_tile), 1, 0).astype(I32)
    vstart = jnp.where(live, starts[e], 0).astype(I32)
    vend = jnp.where(live, ends[e], 0).astype(I32)
    return (tile, e, first, vstart, vend), starts


def _combine_kernel(pos_ref, ys_ref, x1_ref, ga_ref, gb_ref, g2_ref, fg_ref, o_ref, buf_ref, sem, *, tc, final):
    i = pl.program_id(0)

    def row_copy(r, slot):
        p = pos_ref[2 * (i * tc + r) + slot]
        return pltpu.make_async_copy(ys_ref.at[pl.ds(p, 1)], buf_ref.at[slot, pl.ds(r, 1)], sem)

    def start_body(r, carry):
        row_copy(r, 0).start()
        row_copy(r, 1).start()
        return carry

    lax.fori_loop(0, tc, start_body, 0, unroll=8)

    def wait_body(r, carry):
        row_copy(r, 0).wait()
        row_copy(r, 1).wait()
        return carry

    lax.fori_loop(0, tc, wait_body, 0, unroll=8)

    moe = buf_ref[0] * ga_ref[...] + buf_ref[1] * gb_ref[...]
    x2 = x1_ref[...] + g2_ref[...] * moe
    if final:
        x2 = _rms(x2) * fg_ref[...]
    o_ref[...] = x2


def _combine(pos, ys, x1, gate_a, gate_b, mod_l, final_g, seq, final):
    t, d = x1.shape
    tc = min(256, seq)
    per_b = seq // tc
    kern = functools.partial(_combine_kernel, tc=tc, final=final)
    return pl.pallas_call(
        kern,
        out_shape=jax.ShapeDtypeStruct((t, d), F32),
        grid_spec=pltpu.PrefetchScalarGridSpec(
            num_scalar_prefetch=1,
            grid=(t // tc,),
            in_specs=[
                pl.BlockSpec(memory_space=pl.ANY),
                pl.BlockSpec((tc, d), lambda i, pos: (i, 0)),
                pl.BlockSpec((tc, 1), lambda i, pos: (i, 0)),
                pl.BlockSpec((tc, 1), lambda i, pos: (i, 0)),
                pl.BlockSpec((None, None, 1, d), lambda i, pos: (i // per_b, 5, 0, 0)),
                pl.BlockSpec((1, d), lambda i, pos: (0, 0)),
            ],
            out_specs=pl.BlockSpec((tc, d), lambda i, pos: (i, 0)),
            scratch_shapes=[pltpu.VMEM((2, tc, d), F32), pltpu.SemaphoreType.DMA],
        ),
        compiler_params=_cparams(("arbitrary",)),
        name="moe_combine",
    )(pos, ys, x1, gate_a, gate_b, mod_l, final_g.reshape(1, d))


def _rearrange_w_in(w):
    o_kidx = Q_RANK + KV_RANK
    o_mq = o_kidx + IDX_DIM + N_IDX_HEADS
    o_mi = o_mq + 2 * N_MLSTM_HEADS * MLSTM_QK_DIM + 2 * MLSTM_WIDTH
    gates = w[:, o_mi:o_mi + 2 * N_MLSTM_HEADS]
    small = jnp.concatenate([w[:, o_kidx:o_mq], gates], axis=1)
    small = jnp.pad(small, ((0, 0), (0, Z_MQ - Z_KW - small.shape[1])))
    return jnp.concatenate([w[:, :o_kidx], small, w[:, o_mq:o_mi]], axis=1).astype(BF16)


def kernel(x, c, ada_w, ada_b, mix_norm_g, w_in, cq_norm_g, ckv_norm_g, w_uq, w_uk, w_uv, w_qidx,
           mlstm_i_b, mlstm_f_b, mlstm_norm_g, w_out, ffn_norm_g, router_w, router_b, w_gate_up,
           w_down, final_norm_g):
    bsz, seq, d = x.shape
    depth = ada_w.shape[0]
    t = bsz * seq
    assert seq % Q_BLOCK == 0 and seq % min(512, seq) == 0

    mod = _modulation(c, ada_w, ada_b).reshape(depth, bsz, 6, 1, d)
    router_wp = jnp.pad(router_w, ((0, 0), (0, LANES - N_EXPERTS)))
    gmm_tm = 256

    x2d = x.reshape(t, d)
    for l in range(depth):
        mod_l = mod[l]
        z = _in_projection(x2d, mix_norm_g[l], mod_l, _rearrange_w_in(w_in[l]), seq)
        w_ukt = jnp.transpose(w_uk[l], (1, 2, 0)).astype(BF16)
        w_uvh = jnp.transpose(w_uv[l], (1, 0, 2)).astype(BF16)
        w_qidx_p = jnp.pad(w_qidx[l].reshape(Q_RANK, N_IDX_HEADS, IDX_DIM),
                           ((0, 0), (0, 0), (0, LANES - IDX_DIM))).reshape(Q_RANK, N_IDX_HEADS * LANES).astype(BF16)
        qlat, qidx, ckvn, kidx = _dsa_prep(z, cq_norm_g[l].reshape(1, Q_RANK), ckv_norm_g[l].reshape(1, KV_RANK),
                                           w_uq[l].astype(BF16), w_ukt, w_qidx_p, bsz, seq)
        att = _dsa_attention(qidx, qlat, z, kidx, ckvn, w_uvh, bsz, seq)
        gate_bias = jnp.zeros((1, LANES), F32)
        gate_bias = gate_bias.at[0, KW_MI:KW_MI + N_MLSTM_HEADS].set(mlstm_i_b[l])
        gate_bias = gate_bias.at[0, KW_MF:KW_MF + N_MLSTM_HEADS].set(mlstm_f_b[l])
        hm = _mlstm(z, gate_bias, mlstm_norm_g[l].reshape(1, MLSTM_WIDTH), bsz, seq)
        x1, h2, ri, rf, cnt = _outproj_router(att, hm, x2d, mod_l, ffn_norm_g[l], w_out[l].astype(BF16),
                                              router_wp, router_b, seq)
        counts = cnt[:, 0].astype(I32)
        sched, starts = _visit_schedule(counts, 2 * t, gmm_tm)
        e1 = ri[:, 0, :].reshape(t)
        e2 = ri[:, 1, :].reshape(t)
        pos = jnp.stack([starts[e1] + ri[:, 2, :].reshape(t), starts[e2] + ri[:, 3, :].reshape(t)],
                        axis=-1).reshape(2 * t).astype(I32)
        xs = _dispatch(pos, h2)
        ys = _gmm(sched, xs, w_gate_up[l].astype(BF16), w_down[l].astype(BF16), gmm_tm)
        x2d = _combine(pos, ys, x1, rf[:, 0, :].reshape(t, 1), rf[:, 1, :].reshape(t, 1), mod_l,
                       final_norm_g, seq, final=(l == depth - 1))
    return x2d.reshape(bsz, seq, d)
```

```python
import functools

import jax
import jax.numpy as jnp
from jax import lax
from jax.experimental import pallas as pl
from jax.experimental.pallas import tpu as pltpu

F32 = jnp.float32
BF16 = jnp.bfloat16
I32 = jnp.int32

EPS = 1e-6

N_ATT_HEADS = 8
ATT_HEAD_DIM = 128
Q_RANK = 512
KV_RANK = 256
N_IDX_HEADS = 8
IDX_DIM = 64
TOPK_MAX = 256
Q_BLOCK = 128
N_MLSTM_HEADS = 4
MLSTM_QK_DIM = 128
MLSTM_V_DIM = 256
N_EXPERT_GROUPS = 4
EXPERTS_PER_GROUP = 8
N_EXPERTS = N_EXPERT_GROUPS * EXPERTS_PER_GROUP
D_FF_EXPERT = 512
ATT_WIDTH = N_ATT_HEADS * ATT_HEAD_DIM
MLSTM_WIDTH = N_MLSTM_HEADS * MLSTM_V_DIM

LANES = 128
SUBLANES = 8
VMEM_LIMIT = 52 * 1024 * 1024

Z_CQ = 0
Z_CKV = 512
Z_KW = 768
Z_MQ = 1024
Z_MK = 1536
Z_MV = 2048
Z_MO = 3072
Z_WIDTH = 4096
KW_WIDX = 64
KW_MI = 72
KW_MF = 76

NEG_BIG = -1e30
LOG2E = 1.4426950408889634
DSA_KEY_CHUNK = 256
INT_MIN = -2147483648


def _cparams(sem):
    return pltpu.CompilerParams(dimension_semantics=sem, vmem_limit_bytes=VMEM_LIMIT)


def _rms(x):
    return x * lax.rsqrt(jnp.mean(x * x, axis=-1, keepdims=True) + EPS)


def _dot(a, b):
    return jnp.dot(a, b, preferred_element_type=F32)


def _dot_nt(a, b):
    return lax.dot_general(a, b, (((1,), (1,)), ((), ())), preferred_element_type=F32)


def _dot_f32(a, b):
    return jnp.dot(a, b, preferred_element_type=F32, precision=lax.Precision.HIGHEST)


def _mod_kernel(c_ref, w_ref, b_ref, o_ref):
    c = c_ref[...]
    ca = (c * jax.nn.sigmoid(c)).astype(BF16)
    o_ref[...] = _dot(ca, w_ref[...].astype(BF16)) + b_ref[...]


def _modulation(c, ada_w, ada_b):
    depth, d, n = ada_w.shape
    bsz = c.shape[0]
    rows = ((bsz + SUBLANES - 1) // SUBLANES) * SUBLANES
    c_pad = jnp.pad(c, ((0, rows - bsz), (0, 0)))
    tn = 1024
    out = pl.pallas_call(
        _mod_kernel,
        out_shape=jax.ShapeDtypeStruct((depth, rows, n), F32),
        grid=(depth, n // tn),
        in_specs=[
            pl.BlockSpec((rows, d), lambda l, j: (0, 0)),
            pl.BlockSpec((None, d, tn), lambda l, j: (l, 0, j)),
            pl.BlockSpec((None, 1, tn), lambda l, j: (l, 0, j)),
        ],
        out_specs=pl.BlockSpec((None, rows, tn), lambda l, j: (l, 0, j)),
        compiler_params=_cparams(("arbitrary", "arbitrary")),
        name="ada_modulation",
    )(c_pad, ada_w, ada_b.reshape(depth, 1, n))
    return out[:, :bsz]


def _inproj_kernel(x_ref, g_ref, sc_ref, sh_ref, w_ref, z_ref, h_ref):
    @pl.when(pl.program_id(1) == 0)
    def _():
        y = _rms(x_ref[...]) * g_ref[...]
        h_ref[...] = (y * (1.0 + sc_ref[...]) + sh_ref[...]).astype(BF16)

    z_ref[...] = _dot(h_ref[...], w_ref[...])


def _in_projection(x2d, gain, mod_l, w_in_r, seq):
    t, d = x2d.shape
    tm = min(512, seq)
    tn = 1024
    per_b = seq // tm
    return pl.pallas_call(
        _inproj_kernel,
        out_shape=jax.ShapeDtypeStruct((t, Z_WIDTH), F32),
        grid=(t // tm, Z_WIDTH // tn),
        in_specs=[
            pl.BlockSpec((tm, d), lambda i, j: (i, 0)),
            pl.BlockSpec((1, d), lambda i, j: (0, 0)),
            pl.BlockSpec((None, None, 1, d), lambda i, j: (i // per_b, 1, 0, 0)),
            pl.BlockSpec((None, None, 1, d), lambda i, j: (i // per_b, 0, 0, 0)),
            pl.BlockSpec((d, tn), lambda i, j: (0, j)),
        ],
        out_specs=pl.BlockSpec((tm, tn), lambda i, j: (i, j)),
        scratch_shapes=[pltpu.VMEM((tm, d), BF16)],
        compiler_params=_cparams(("arbitrary", "arbitrary")),
        name="norm_in_projection",
    )(x2d, gain.reshape(1, d), mod_l, mod_l, w_in_r)


def _dsa_prep_kernel(cq_ref, ckv_ref, kw_ref, gq_ref, gkv_ref, wuq_ref, wuk_ref, wqi_ref,
                     qlt_ref, qidx_ref, ckvn_ref, ckvt_ref, kidx_ref, *, tm, kc):
    qb = Q_BLOCK
    cqn = (_rms(cq_ref[...]) * gq_ref[...]).astype(BF16)
    ckvn = _rms(ckv_ref[...]) * gkv_ref[...]
    ckvn_ref[...] = ckvn.astype(BF16)
    ckvt = jnp.transpose(ckvn).astype(BF16)
    for j in range(tm // kc):
        ckvt_ref[j] = ckvt[:, j * kc:(j + 1) * kc]
    q = _dot(cqn, wuq_ref[...]).astype(BF16)
    qi = _dot(cqn, wqi_ref[...])
    qscale = ATT_HEAD_DIM ** -0.5 * LOG2E
    for h in range(N_ATT_HEADS):
        qlt = _dot_nt(wuk_ref[h], q[:, h * ATT_HEAD_DIM:(h + 1) * ATT_HEAD_DIM]) * qscale
        for j in range(tm // qb):
            qlt_ref[j, :, h * qb:(h + 1) * qb] = qlt[:, j * qb:(j + 1) * qb].astype(BF16)
    for h in range(N_IDX_HEADS):
        qidx_ref[h] = qi[:, h * LANES:(h + 1) * LANES].astype(BF16)
    kw = kw_ref[...]
    lane = lax.broadcasted_iota(I32, kw.shape, 1)
    kidx_ref[...] = jnp.where(lane < IDX_DIM, kw, 0.0).astype(BF16)


def _dsa_prep(z, gq, gkv, w_uq, w_ukh, w_qidx_p, bsz, seq):
    t = z.shape[0]
    tm = min(256, seq)
    kc = min(DSA_KEY_CHUNK, seq)
    qb = Q_BLOCK
    per_b = seq // tm
    kern = functools.partial(_dsa_prep_kernel, tm=tm, kc=kc)
    return pl.pallas_call(
        kern,
        out_shape=(
            jax.ShapeDtypeStruct((bsz, seq // qb, KV_RANK, N_ATT_HEADS * qb), BF16),
            jax.ShapeDtypeStruct((bsz, N_IDX_HEADS, seq, LANES), BF16),
            jax.ShapeDtypeStruct((t, KV_RANK), BF16),
            jax.ShapeDtypeStruct((bsz, seq // kc, KV_RANK, kc), BF16),
            jax.ShapeDtypeStruct((t, LANES), BF16),
        ),
        grid=(t // tm,),
        in_specs=[
            pl.BlockSpec((tm, Q_RANK), lambda i: (i, Z_CQ // Q_RANK)),
            pl.BlockSpec((tm, KV_RANK), lambda i: (i, Z_CKV // KV_RANK)),
            pl.BlockSpec((tm, LANES), lambda i: (i, Z_KW // LANES)),
            pl.BlockSpec((1, Q_RANK), lambda i: (0, 0)),
            pl.BlockSpec((1, KV_RANK), lambda i: (0, 0)),
            pl.BlockSpec((Q_RANK, ATT_WIDTH), lambda i: (0, 0)),
            pl.BlockSpec((N_ATT_HEADS, KV_RANK, ATT_HEAD_DIM), lambda i: (0, 0, 0)),
            pl.BlockSpec((Q_RANK, N_IDX_HEADS * LANES), lambda i: (0, 0)),
        ],
        out_specs=(
            pl.BlockSpec((None, tm // qb, KV_RANK, N_ATT_HEADS * qb), lambda i: (i // per_b, i % per_b, 0, 0)),
            pl.BlockSpec((None, N_IDX_HEADS, tm, LANES), lambda i: (i // per_b, 0, i % per_b, 0)),
            pl.BlockSpec((tm, KV_RANK), lambda i: (i, 0)),
            pl.BlockSpec((None, tm // kc, KV_RANK, kc), lambda i: (i // per_b, i % per_b, 0, 0)),
            pl.BlockSpec((tm, LANES), lambda i: (i, 0)),
        ),
        compiler_params=_cparams(("arbitrary",)),
        name="dsa_prep",
    )(z, z, z, gq, gkv, w_uq, w_ukh, w_qidx_p)


def _dsa_attn_kernel(qidx_ref, qlt_ref, kw_ref, kidx_ref, ckvn_ref, ckvt_ref, wuvt_ref, o_ref,
                     key_ref, m_ref, l_ref, acc_ref, lo_ref, jl_ref, s_ref, *, kc, k_sel, seq):
    qb = Q_BLOCK
    i = pl.program_id(1)
    nkeys = (i + 1) * qb
    nch = (nkeys + kc - 1) // kc
    t_row = i * qb + lax.broadcasted_iota(I32, (1, qb), 1)
    sub_pos = lax.broadcasted_iota(I32, (kc, qb), 0)
    kwt = jnp.transpose(kw_ref[...]) * (N_IDX_HEADS ** -0.5 * IDX_DIM ** -0.5)

    def idx_body(c, carry):
        ks = kidx_ref[pl.ds(pl.multiple_of(c * kc, kc), kc), :]
        acc = jnp.zeros((kc, qb), F32)
        for g in range(N_IDX_HEADS // 2):
            r = _dot_nt(ks, qidx_ref[2 * g:2 * g + 2].reshape(2 * qb, LANES))
            for u in range(2):
                h = 2 * g + u
                acc = acc + kwt[KW_WIDX + h:KW_WIDX + h + 1, :] * jnp.maximum(r[:, u * qb:(u + 1) * qb], 0.0)
        pos = c * kc + sub_pos
        acc = jnp.where(pos <= t_row, acc + 0.0, -jnp.inf)
        bits = pltpu.bitcast(acc, I32)
        key_ref[c] = bits ^ ((bits >> 31) & 0x7FFFFFFF)
        return carry

    lax.fori_loop(0, nch, idx_body, 0)

    npair = (nch + 1) // 2

    @pl.when(nch % 2 == 1)
    def _():
        key_ref[nch] = jnp.full((kc, qb), INT_MIN, I32)

    nacc = 4 * SUBLANES

    def count(pred_fn):
        def body(cp, acc):
            for u in range(2):
                c = 2 * cp + u
                ind = jnp.where(pred_fn(key_ref[c], c * kc + sub_pos), 1.0, 0.0)
                acc = acc + jnp.sum(ind.reshape(kc // nacc, nacc, qb), axis=0)
            return acc
        acc = lax.fori_loop(0, npair, body, jnp.zeros((nacc, qb), F32))
        return jnp.sum(acc, axis=0, keepdims=True)

    kf = float(k_sel)

    def bit_body(it, lo):
        cand = lo + lax.shift_left(jnp.int32(1), jnp.int32(31) - it)
        cnt = count(lambda k, pos: k >= cand)
        return jnp.where(cnt >= kf, cand, lo)

    lo = lax.fori_loop(0, 32, bit_body, jnp.full((1, qb), INT_MIN, I32))
    lo_ref[...] = lo

    cnt_gt = count(lambda k, pos: k > lo)
    cnt_eq = count(lambda k, pos: k == lo)
    need = kf - cnt_gt
    neg_inf_key = INT_MIN + 0x7FFFFF
    bad = jnp.where((cnt_eq > need) & (lo > neg_inf_key), 1.0, 0.0)
    jl_ref[...] = jnp.full((1, qb), seq, I32)

    @pl.when(jnp.max(bad) > 0.0)
    def _():
        nbits = max(1, (seq - 1).bit_length())

        def jbit_body(it, v):
            cand = v + lax.shift_left(jnp.int32(1), jnp.int32(nbits - 1) - it)
            cnt = count(lambda k, pos: (k == lo) & (pos < cand))
            return jnp.where(cnt < need, cand, v)

        jl_ref[...] = lax.fori_loop(0, nbits, jbit_body, jnp.zeros((1, qb), I32))

    m_ref[...] = jnp.full(m_ref.shape, NEG_BIG, F32)
    l_ref[...] = jnp.zeros(l_ref.shape, F32)
    acc_ref[...] = jnp.zeros(acc_ref.shape, F32)
    hg = 2 * qb

    def att_body(c, carry):
        kv = ckvn_ref[pl.ds(pl.multiple_of(c * kc, kc), kc), :]
        kvt = ckvt_ref[c]
        key = key_ref[c]
        pos = c * kc + sub_pos
        lo_c = lo_ref[...]
        tie = jnp.where(key == lo_c, jnp.where(pos <= jl_ref[...], 0.0, NEG_BIG), NEG_BIG)
        bias = jnp.where(pos <= t_row, jnp.where(key > lo_c, 0.0, tie), NEG_BIG)
        bias2 = jnp.concatenate([bias, bias], axis=1)
        for g in range(N_ATT_HEADS // 2):
            cols = slice(g * hg, (g + 1) * hg)
            s_ref[g] = _dot(kv, qlt_ref[:, cols]) + bias2
            m_old = m_ref[:, cols]
            m_new = jnp.maximum(m_old, jnp.max(s_ref[g], axis=0, keepdims=True))
            alpha = jnp.exp2(m_old - m_new)
            p = jnp.exp2(s_ref[g] - m_new)
            l_ref[:, cols] = alpha * l_ref[:, cols] + jnp.sum(p, axis=0, keepdims=True)
            m_ref[:, cols] = m_new
            acc_ref[:, cols] = acc_ref[:, cols] * alpha + _dot(kvt, p.astype(BF16))
        return carry

    lax.fori_loop(0, nch, att_body, 0)

    o_t = (acc_ref[...] * (1.0 / l_ref[...])).astype(BF16)
    for h in range(N_ATT_HEADS):
        ot_h = _dot(wuvt_ref[h], o_t[:, h * qb:(h + 1) * qb])
        o_ref[:, h * ATT_HEAD_DIM:(h + 1) * ATT_HEAD_DIM] = jnp.transpose(ot_h).astype(o_ref.dtype)


def _dsa_attention(qidx, qlt, z, kidx, ckvn, ckvt, w_uvt, bsz, seq):
    t = z.shape[0]
    qb = Q_BLOCK
    kc = min(DSA_KEY_CHUNK, seq)
    nb = seq // qb
    k_sel = min(TOPK_MAX, seq // 4)
    kern = functools.partial(_dsa_attn_kernel, kc=kc, k_sel=k_sel, seq=seq)
    return pl.pallas_call(
        kern,
        out_shape=jax.ShapeDtypeStruct((t, ATT_WIDTH), BF16),
        grid=(bsz, nb),
        in_specs=[
            pl.BlockSpec((None, N_IDX_HEADS, qb, LANES), lambda b, i: (b, 0, i, 0)),
            pl.BlockSpec((None, None, KV_RANK, N_ATT_HEADS * qb), lambda b, i: (b, i, 0, 0)),
            pl.BlockSpec((qb, LANES), lambda b, i: (b * nb + i, Z_KW // LANES)),
            pl.BlockSpec((seq, LANES), lambda b, i: (b, 0)),
            pl.BlockSpec((seq, KV_RANK), lambda b, i: (b, 0)),
            pl.BlockSpec((None, seq // kc, KV_RANK, kc), lambda b, i: (b, 0, 0, 0)),
            pl.BlockSpec((N_ATT_HEADS, ATT_HEAD_DIM, KV_RANK), lambda b, i: (0, 0, 0)),
        ],
        out_specs=pl.BlockSpec((qb, ATT_WIDTH), lambda b, i: (b * nb + i, 0)),
        scratch_shapes=[
            pltpu.VMEM((seq // kc + 1, kc, qb), I32),
            pltpu.VMEM((1, N_ATT_HEADS * qb), F32),
            pltpu.VMEM((1, N_ATT_HEADS * qb), F32),
            pltpu.VMEM((KV_RANK, N_ATT_HEADS * qb), F32),
            pltpu.VMEM((1, qb), I32),
            pltpu.VMEM((1, qb), I32),
            pltpu.VMEM((N_ATT_HEADS // 2, kc, 2 * qb), F32),
        ],
        compiler_params=_cparams(("arbitrary", "arbitrary")),
        name="dsa_attention",
    )(qidx, qlt, z, kidx, ckvn, ckvt, w_uvt)


def _log_sigmoid(x):
    return jnp.minimum(x, 0.0) - jnp.log(1.0 + jnp.exp(-jnp.abs(x)))


def _mlstm_kernel(q_ref, k_ref, v_ref, og_ref, kw_ref, gb_ref, gn_ref, out_ref,
                  c_ref, n_ref, m_ref, *, chunk):
    lc = chunk
    dk, dv = MLSTM_QK_DIM, MLSTM_V_DIM

    @pl.when(pl.program_id(1) == 0)
    def _():
        c_ref[...] = jnp.zeros(c_ref.shape, F32)
        n_ref[...] = jnp.zeros(n_ref.shape, F32)
        m_ref[...] = jnp.zeros(m_ref.shape, F32)

    kwb = kw_ref[...] + gb_ref[...]
    kwt = jnp.transpose(kwb)
    row = lax.broadcasted_iota(I32, (lc, lc), 0)
    col = lax.broadcasted_iota(I32, (lc, lc), 1)
    causal = col <= row
    tri = jnp.where(causal, 1.0, 0.0)
    tri_t = jnp.where(row <= col, 1.0, 0.0)
    cum_col = _dot_f32(tri, _log_sigmoid(kwb))
    cum_row = _dot_f32(_log_sigmoid(kwt), tri_t)

    for h in range(N_MLSTM_HEADS):
        i_col = kwb[:, KW_MI + h:KW_MI + h + 1]
        i_row = kwt[KW_MI + h:KW_MI + h + 1, :]
        b_col = cum_col[:, KW_MF + h:KW_MF + h + 1]
        b_row = cum_row[KW_MF + h:KW_MF + h + 1, :]
        m_prev = m_ref[h]

        dmat = jnp.where(causal, b_col - b_row + i_row, -jnp.inf)
        m_inter = b_col + m_prev
        m_j = jnp.maximum(m_inter, jnp.max(dmat, axis=1, keepdims=True))
        w_inter = jnp.exp(m_inter - m_j)

        qh = q_ref[:, h * dk:(h + 1) * dk] * (dk ** -0.5)
        kh = k_ref[:, h * dk:(h + 1) * dk]
        vb = v_ref[:, h * dv:(h + 1) * dv].astype(BF16)
        qb = qh.astype(BF16)
        sc = _dot_nt(qb, kh.astype(BF16)) * jnp.exp(dmat - m_j)
        c_old = c_ref[h]
        n_old = n_ref[h]
        num = w_inter * _dot(qb, c_old.astype(BF16)) + _dot(sc.astype(BF16), vb)
        den = (w_inter * jnp.sum(qh * n_old, axis=1, keepdims=True)
               + jnp.sum(sc, axis=1, keepdims=True))
        hval = num / jnp.maximum(jnp.abs(den), jnp.exp(-m_j))

        hn = _rms(hval) * gn_ref[:, h * dv:(h + 1) * dv]
        gate = jax.nn.sigmoid(og_ref[:, h * dv:(h + 1) * dv])
        out_ref[:, h * dv:(h + 1) * dv] = (hn * gate).astype(out_ref.dtype)

        b_last = b_col[lc - 1:lc, :]
        g_col = b_last - b_col + i_col
        m_new = jnp.maximum(b_last + m_prev, jnp.max(g_col, axis=0, keepdims=True))
        w_old = jnp.exp(b_last + m_prev - m_new)
        kwgt = kh * jnp.exp(g_col - m_new)
        c_ref[h] = w_old * c_old + _dot(jnp.transpose(kwgt).astype(BF16), vb)
        n_ref[h] = w_old * n_old + jnp.sum(kwgt, axis=0, keepdims=True)
        m_ref[h] = m_new


def _mlstm(z, gate_bias, norm_g, bsz, seq):
    t = z.shape[0]
    lc = min(256, seq)
    nc = seq // lc
    qk_w = N_MLSTM_HEADS * MLSTM_QK_DIM
    kern = functools.partial(_mlstm_kernel, chunk=lc)
    row = lambda b, c: b * nc + c
    return pl.pallas_call(
        kern,
        out_shape=jax.ShapeDtypeStruct((t, MLSTM_WIDTH), BF16),
        grid=(bsz, nc),
        in_specs=[
            pl.BlockSpec((lc, qk_w), lambda b, c: (row(b, c), Z_MQ // qk_w)),
            pl.BlockSpec((lc, qk_w), lambda b, c: (row(b, c), Z_MK // qk_w)),
            pl.BlockSpec((lc, MLSTM_WIDTH), lambda b, c: (row(b, c), Z_MV // MLSTM_WIDTH)),
            pl.BlockSpec((lc, MLSTM_WIDTH), lambda b, c: (row(b, c), Z_MO // MLSTM_WIDTH)),
            pl.BlockSpec((lc, LANES), lambda b, c: (row(b, c), Z_KW // LANES)),
            pl.BlockSpec((1, LANES), lambda b, c: (0, 0)),
            pl.BlockSpec((1, MLSTM_WIDTH), lambda b, c: (0, 0)),
        ],
        out_specs=pl.BlockSpec((lc, MLSTM_WIDTH), lambda b, c: (row(b, c), 0)),
        scratch_shapes=[
            pltpu.VMEM((N_MLSTM_HEADS, MLSTM_QK_DIM, MLSTM_V_DIM), F32),
            pltpu.VMEM((N_MLSTM_HEADS, 1, MLSTM_QK_DIM), F32),
            pltpu.VMEM((N_MLSTM_HEADS, 1, 1), F32),
        ],
        compiler_params=_cparams(("arbitrary", "arbitrary")),
        name="mlstm",
    )(z, z, z, z, z, gate_bias, norm_g)


def _outproj_router_kernel(att_ref, hm_ref, x_ref, g1_ref, gn_ref, sc_ref, sh_ref, wo_ref, rw_ref, rb_ref,
                           x1_ref, h2_ref, ri_ref, rf_ref, cnt_ref, carry_ref, *, tm):
    @pl.when(pl.program_id(0) == 0)
    def _():
        carry_ref[...] = jnp.zeros(carry_ref.shape, F32)

    y = _dot(att_ref[...], wo_ref[:ATT_WIDTH, :]) + _dot(hm_ref[...], wo_ref[ATT_WIDTH:, :])
    x1 = x_ref[...] + g1_ref[...] * y
    x1_ref[...] = x1
    h2 = _rms(x1) * gn_ref[...] * (1.0 + sc_ref[...]) + sh_ref[...]
    h2_ref[...] = h2

    logits = _dot_f32(h2, rw_ref[...])
    aff = jax.nn.sigmoid(jnp.transpose(logits)[:N_EXPERTS, :])
    sel = aff + rb_ref[...]

    epg = EXPERTS_PER_GROUP
    riota = lax.broadcasted_iota(I32, (epg, tm), 0)
    best = None
    for g in range(N_EXPERT_GROUPS):
        v = sel[g * epg:(g + 1) * epg]
        a = aff[g * epg:(g + 1) * epg]
        m1 = jnp.max(v, axis=0, keepdims=True)
        i1 = jnp.min(jnp.where(v == m1, riota, epg), axis=0, keepdims=True)
        v2 = jnp.where(riota == i1, -jnp.inf, v)
        m2 = jnp.max(v2, axis=0, keepdims=True)
        i2 = jnp.min(jnp.where(v2 == m2, riota, epg), axis=0, keepdims=True)
        a1 = jnp.sum(jnp.where(riota == i1, a, 0.0), axis=0, keepdims=True)
        a2 = jnp.sum(jnp.where(riota == i2, a, 0.0), axis=0, keepdims=True)
        cur = (m1 + m2, i1 + g * epg, i2 + g * epg, a1, a2)
        if best is None:
            best = cur
        else:
            better = cur[0] > best[0]
            best = tuple(jnp.where(better, c_, b_) for c_, b_ in zip(cur, best))
    _, e1, e2, a1, a2 = best
    asum = a1 + a2

    eiota = lax.broadcasted_iota(I32, (N_EXPERTS, tm), 0)
    hit1 = eiota == e1
    hit2 = eiota == e2
    onehot = jnp.where(hit1, 1.0, jnp.where(hit2, 1.0, 0.0))
    srow = lax.broadcasted_iota(I32, (tm, tm), 0)
    scol = lax.broadcasted_iota(I32, (tm, tm), 1)
    before = jnp.where(srow < scol, 1.0, 0.0).astype(BF16)
    rank = _dot(onehot.astype(BF16), before) + carry_ref[...]
    r1 = jnp.sum(jnp.where(hit1, rank, 0.0), axis=0, keepdims=True)
    r2 = jnp.sum(jnp.where(hit2, rank, 0.0), axis=0, keepdims=True)
    carry = carry_ref[...] + jnp.sum(onehot, axis=1, keepdims=True)
    carry_ref[...] = carry
    cnt_ref[...] = jnp.broadcast_to(carry, cnt_ref.shape)

    zi = jnp.zeros((1, tm), I32)
    ri_ref[...] = jnp.concatenate(
        [e1, e2, r1.astype(I32), r2.astype(I32), zi, zi, zi, zi], axis=0)
    zf = jnp.zeros((1, tm), F32)
    rf_ref[...] = jnp.concatenate([a1 / asum, a2 / asum, zf, zf, zf, zf, zf, zf], axis=0)


def _outproj_router(att, hm, x2d, mod_l, gain, w_out_b, router_wp, router_b, seq):
    t, d = x2d.shape
    tm = min(256, seq)
    per_b = seq // tm
    nt = t // tm
    kern = functools.partial(_outproj_router_kernel, tm=tm)
    mod_spec = lambda which: pl.BlockSpec((None, None, 1, d), lambda i: (i // per_b, which, 0, 0))
    return pl.pallas_call(
        kern,
        out_shape=(
            jax.ShapeDtypeStruct((t, d), F32),
            jax.ShapeDtypeStruct((t, d), F32),
            jax.ShapeDtypeStruct((nt, SUBLANES, tm), I32),
            jax.ShapeDtypeStruct((nt, SUBLANES, tm), F32),
            jax.ShapeDtypeStruct((N_EXPERTS, LANES), F32),
        ),
        grid=(nt,),
        in_specs=[
            pl.BlockSpec((tm, ATT_WIDTH), lambda i: (i, 0)),
            pl.BlockSpec((tm, MLSTM_WIDTH), lambda i: (i, 0)),
            pl.BlockSpec((tm, d), lambda i: (i, 0)),
            mod_spec(2),
            pl.BlockSpec((1, d), lambda i: (0, 0)),
            mod_spec(4),
            mod_spec(3),
            pl.BlockSpec((ATT_WIDTH + MLSTM_WIDTH, d), lambda i: (0, 0)),
            pl.BlockSpec((d, LANES), lambda i: (0, 0)),
            pl.BlockSpec((N_EXPERTS, 1), lambda i: (0, 0)),
        ],
        out_specs=(
            pl.BlockSpec((tm, d), lambda i: (i, 0)),
            pl.BlockSpec((tm, d), lambda i: (i, 0)),
            pl.BlockSpec((None, SUBLANES, tm), lambda i: (i, 0, 0)),
            pl.BlockSpec((None, SUBLANES, tm), lambda i: (i, 0, 0)),
            pl.BlockSpec((N_EXPERTS, LANES), lambda i: (0, 0)),
        ),
        scratch_shapes=[pltpu.VMEM((N_EXPERTS, 1), F32)],
        compiler_params=_cparams(("arbitrary",)),
        name="outproj_router",
    )(att, hm, x2d, mod_l, gain.reshape(1, d), mod_l, mod_l, w_out_b, router_wp, router_b.reshape(N_EXPERTS, 1))


def _dispatch_kernel(pos_ref, src_ref, dst_ref, sem, *, tc):
    i = pl.program_id(0)

    def row_copy(r, slot):
        p = pos_ref[2 * (i * tc + r) + slot]
        return pltpu.make_async_copy(src_ref.at[pl.ds(r, 1)], dst_ref.at[pl.ds(p, 1)], sem)

    def start_body(r, carry):
        row_copy(r, 0).start()
        row_copy(r, 1).start()
        return carry

    lax.fori_loop(0, tc, start_body, 0, unroll=8)

    def wait_body(r, carry):
        row_copy(r, 0).wait()
        row_copy(r, 1).wait()
        return carry

    lax.fori_loop(0, tc, wait_body, 0, unroll=8)


def _dispatch(pos, h2, seq):
    t, d = h2.shape
    tc = min(256, seq)
    kern = functools.partial(_dispatch_kernel, tc=tc)
    return pl.pallas_call(
        kern,
        out_shape=jax.ShapeDtypeStruct((2 * t, d), h2.dtype),
        grid_spec=pltpu.PrefetchScalarGridSpec(
            num_scalar_prefetch=1,
            grid=(t // tc,),
            in_specs=[pl.BlockSpec((tc, d), lambda i, pos: (i, 0))],
            out_specs=pl.BlockSpec(memory_space=pl.ANY),
            scratch_shapes=[pltpu.SemaphoreType.DMA],
        ),
        compiler_params=pltpu.CompilerParams(dimension_semantics=("arbitrary",), has_side_effects=True,
                                             vmem_limit_bytes=VMEM_LIMIT),
        name="moe_dispatch",
    )(pos, h2)


def _gmm_kernel(vt_ref, ve_ref, vfirst_ref, vstart_ref, vend_ref, xs_ref, wgu_ref, wd_ref, ys_ref, *, tm):
    v = pl.program_id(0)
    start = vstart_ref[v]
    end = vend_ref[v]

    @pl.when(end > start)
    def _():
        x = xs_ref[...].astype(BF16)
        gu = _dot(x, wgu_ref[...])
        gt = gu[:, :D_FF_EXPERT]
        up = gu[:, D_FF_EXPERT:]
        act = (gt * jax.nn.sigmoid(gt)) * up
        y = _dot(act.astype(BF16), wd_ref[...])
        rows = vt_ref[v] * tm + lax.broadcasted_iota(I32, (tm, 1), 0)
        mine = (rows >= start) & (rows < end)

        @pl.when(vfirst_ref[v] == 1)
        def _():
            ys_ref[...] = jnp.where(mine, y, 0.0)

        @pl.when(vfirst_ref[v] == 0)
        def _():
            ys_ref[...] = jnp.where(mine, y, ys_ref[...])


def _gmm(sched, xs, wgu_b, wd_b, tm):
    rows, d = xs.shape
    n_visits = sched[0].shape[0]
    kern = functools.partial(_gmm_kernel, tm=tm)
    return pl.pallas_call(
        kern,
        out_shape=jax.ShapeDtypeStruct((rows, d), F32),
        grid_spec=pltpu.PrefetchScalarGridSpec(
            num_scalar_prefetch=5,
            grid=(n_visits,),
            in_specs=[
                pl.BlockSpec((tm, d), lambda v, vt, ve, vf, vs, vn: (vt[v], 0)),
                pl.BlockSpec((None, d, 2 * D_FF_EXPERT), lambda v, vt, ve, vf, vs, vn: (ve[v], 0, 0)),
                pl.BlockSpec((None, D_FF_EXPERT, d), lambda v, vt, ve, vf, vs, vn: (ve[v], 0, 0)),
            ],
            out_specs=pl.BlockSpec((tm, d), lambda v, vt, ve, vf, vs, vn: (vt[v], 0)),
        ),
        compiler_params=_cparams(("arbitrary",)),
        name="moe_experts",
    )(*sched, xs, wgu_b, wd_b)


def _visit_schedule(counts, n_rows, tm):
    ends = jnp.cumsum(counts)
    starts = ends - counts
    n_tiles = n_rows // tm
    n_visits = n_tiles + N_EXPERTS - 1
    first_tile = starts // tm
    last_tile = jnp.maximum(ends - 1, 0) // tm
    nv = jnp.where(counts > 0, last_tile - first_tile + 1, 0)
    vend_cum = jnp.cumsum(nv)
    voff = vend_cum - nv
    total = vend_cum[-1]
    v = jnp.arange(n_visits, dtype=I32)
    live = v < total
    e = jnp.searchsorted(vend_cum, jnp.minimum(v, total - 1), side="right").astype(I32)
    e = jnp.minimum(e, N_EXPERTS - 1)
    tile = jnp.where(live, first_tile[e] + (v - voff[e]), n_tiles - 1).astype(I32)
    prev_tile = jnp.concatenate([jnp.full((1,), -1, I32), tile[:-1]])
    first = jnp.where(live & (tile != prev_tile), 1, 0).astype(I32)
    vstart = jnp.where(live, starts[e], 0).astype(I32)
    vend = jnp.where(live, ends[e], 0).astype(I32)
    return (tile, e, first, vstart, vend), starts


def _combine_kernel(pos_ref, ys_ref, x1_ref, ga_ref, gb_ref, g2_ref, fg_ref, o_ref, buf_ref, sem, *, tc, final):
    i = pl.program_id(0)

    def row_copy(r, slot):
        p = pos_ref[2 * (i * tc + r) + slot]
        return pltpu.make_async_copy(ys_ref.at[pl.ds(p, 1)], buf_ref.at[slot, pl.ds(r, 1)], sem)

    def start_body(r, carry):
        row_copy(r, 0).start()
        row_copy(r, 1).start()
        return carry

    lax.fori_loop(0, tc, start_body, 0, unroll=8)

    def wait_body(r, carry):
        row_copy(r, 0).wait()
        row_copy(r, 1).wait()
        return carry

    lax.fori_loop(0, tc, wait_body, 0, unroll=8)

    moe = buf_ref[0] * ga_ref[...] + buf_ref[1] * gb_ref[...]
    x2 = x1_ref[...] + g2_ref[...] * moe
    if final:
        x2 = _rms(x2) * fg_ref[...]
    o_ref[...] = x2


def _combine(pos, ys, x1, gate_a, gate_b, mod_l, final_g, seq, final):
    t, d = x1.shape
    tc = min(256, seq)
    per_b = seq // tc
    kern = functools.partial(_combine_kernel, tc=tc, final=final)
    return pl.pallas_call(
        kern,
        out_shape=jax.ShapeDtypeStruct((t, d), F32),
        grid_spec=pltpu.PrefetchScalarGridSpec(
            num_scalar_prefetch=1,
            grid=(t // tc,),
            in_specs=[
                pl.BlockSpec(memory_space=pl.ANY),
                pl.BlockSpec((tc, d), lambda i, pos: (i, 0)),
                pl.BlockSpec((tc, 1), lambda i, pos: (i, 0)),
                pl.BlockSpec((tc, 1), lambda i, pos: (i, 0)),
                pl.BlockSpec((None, None, 1, d), lambda i, pos: (i // per_b, 5, 0, 0)),
                pl.BlockSpec((1, d), lambda i, pos: (0, 0)),
            ],
            out_specs=pl.BlockSpec((tc, d), lambda i, pos: (i, 0)),
            scratch_shapes=[pltpu.VMEM((2, tc, d), F32), pltpu.SemaphoreType.DMA],
        ),
        compiler_params=_cparams(("arbitrary",)),
        name="moe_combine",
    )(pos, ys, x1, gate_a, gate_b, mod_l, final_g.reshape(1, d))


def _rearrange_w_in(w):
    o_kidx = Q_RANK + KV_RANK
    o_mq = o_kidx + IDX_DIM + N_IDX_HEADS
    o_mi = o_mq + 2 * N_MLSTM_HEADS * MLSTM_QK_DIM + 2 * MLSTM_WIDTH
    gates = w[:, o_mi:o_mi + 2 * N_MLSTM_HEADS]
    small = jnp.concatenate([w[:, o_kidx:o_mq], gates], axis=1)
    small = jnp.pad(small, ((0, 0), (0, Z_MQ - Z_KW - small.shape[1])))
    return jnp.concatenate([w[:, :o_kidx], small, w[:, o_mq:o_mi]], axis=1).astype(BF16)


def kernel(x, c, ada_w, ada_b, mix_norm_g, w_in, cq_norm_g, ckv_norm_g, w_uq, w_uk, w_uv, w_qidx,
           mlstm_i_b, mlstm_f_b, mlstm_norm_g, w_out, ffn_norm_g, router_w, router_b, w_gate_up,
           w_down, final_norm_g):
    bsz, seq, d = x.shape
    depth = ada_w.shape[0]
    t = bsz * seq
    assert seq % Q_BLOCK == 0 and seq % min(512, seq) == 0

    mod = _modulation(c, ada_w, ada_b).reshape(depth, bsz, 6, 1, d)
    router_wp = jnp.pad(router_w, ((0, 0), (0, LANES - N_EXPERTS)))
    gmm_tm = 256

    x2d = x.reshape(t, d)
    for l in range(depth):
        mod_l = mod[l]
        z = _in_projection(x2d, mix_norm_g[l], mod_l, _rearrange_w_in(w_in[l]), seq)
        w_ukh = jnp.transpose(w_uk[l], (1, 0, 2)).astype(BF16)
        w_uvt = jnp.transpose(w_uv[l], (1, 2, 0)).astype(BF16)
        w_qidx_p = jnp.pad(w_qidx[l].reshape(Q_RANK, N_IDX_HEADS, IDX_DIM),
                           ((0, 0), (0, 0), (0, LANES - IDX_DIM))).reshape(Q_RANK, N_IDX_HEADS * LANES).astype(BF16)
        qlt, qidx, ckvn, ckvt, kidx = _dsa_prep(z, cq_norm_g[l].reshape(1, Q_RANK),
                                                ckv_norm_g[l].reshape(1, KV_RANK),
                                                w_uq[l].astype(BF16), w_ukh, w_qidx_p, bsz, seq)
        att = _dsa_attention(qidx, qlt, z, kidx, ckvn, ckvt, w_uvt, bsz, seq)
        gate_bias = jnp.zeros((1, LANES), F32)
        gate_bias = gate_bias.at[0, KW_MI:KW_MI + N_MLSTM_HEADS].set(mlstm_i_b[l])
        gate_bias = gate_bias.at[0, KW_MF:KW_MF + N_MLSTM_HEADS].set(mlstm_f_b[l])
        hm = _mlstm(z, gate_bias, mlstm_norm_g[l].reshape(1, MLSTM_WIDTH), bsz, seq)
        x1, h2, ri, rf, cnt = _outproj_router(att, hm, x2d, mod_l, ffn_norm_g[l], w_out[l].astype(BF16),
                                              router_wp, router_b, seq)
        counts = cnt[:, 0].astype(I32)
        sched, starts = _visit_schedule(counts, 2 * t, gmm_tm)
        e1 = ri[:, 0, :].reshape(t)
        e2 = ri[:, 1, :].reshape(t)
        pos = jnp.stack([starts[e1] + ri[:, 2, :].reshape(t), starts[e2] + ri[:, 3, :].reshape(t)],
                        axis=-1).reshape(2 * t).astype(I32)
        xs = _dispatch(pos, h2, seq)
        ys = _gmm(sched, xs, w_gate_up[l].astype(BF16), w_down[l].astype(BF16), gmm_tm)
        x2d = _combine(pos, ys, x1, rf[:, 0, :].reshape(t, 1), rf[:, 1, :].reshape(t, 1), mod_l,
                       final_norm_g, seq, final=(l == depth - 1))
    return x2d.reshape(bsz, seq, d)
```

```python
import functools

import jax
import jax.numpy as jnp
from jax import lax
from jax.experimental import pallas as pl
from jax.experimental.pallas import tpu as pltpu

F32 = jnp.float32
BF16 = jnp.bfloat16
I32 = jnp.int32

EPS = 1e-6

N_ATT_HEADS = 8
ATT_HEAD_DIM = 128
Q_RANK = 512
KV_RANK = 256
N_IDX_HEADS = 8
IDX_DIM = 64
TOPK_MAX = 256
Q_BLOCK = 128
N_MLSTM_HEADS = 4
MLSTM_QK_DIM = 128
MLSTM_V_DIM = 256
N_EXPERT_GROUPS = 4
EXPERTS_PER_GROUP = 8
N_EXPERTS = N_EXPERT_GROUPS * EXPERTS_PER_GROUP
D_FF_EXPERT = 512
ATT_WIDTH = N_ATT_HEADS * ATT_HEAD_DIM
MLSTM_WIDTH = N_MLSTM_HEADS * MLSTM_V_DIM

LANES = 128
SUBLANES = 8
VMEM_LIMIT = 52 * 1024 * 1024

Z_CQ = 0
Z_CKV = 512
Z_KW = 768
Z_MQ = 1024
Z_MK = 1536
Z_MV = 2048
Z_MO = 3072
Z_WIDTH = 4096
KW_WIDX = 64
KW_MI = 72
KW_MF = 76

NEG_BIG = -1e30
LOG2E = 1.4426950408889634
DSA_KEY_CHUNK = 256
INT_MIN = -2147483648


def _cparams(sem):
    return pltpu.CompilerParams(dimension_semantics=sem, vmem_limit_bytes=VMEM_LIMIT)


def _rms(x):
    return x * lax.rsqrt(jnp.mean(x * x, axis=-1, keepdims=True) + EPS)


def _dot(a, b):
    return jnp.dot(a, b, preferred_element_type=F32)


def _dot_nt(a, b):
    return lax.dot_general(a, b, (((1,), (1,)), ((), ())), preferred_element_type=F32)


def _dot_f32(a, b):
    return jnp.dot(a, b, preferred_element_type=F32, precision=lax.Precision.HIGHEST)


def _mod_kernel(c_ref, w_ref, b_ref, o_ref):
    c = c_ref[...]
    ca = (c * jax.nn.sigmoid(c)).astype(BF16)
    o_ref[...] = _dot(ca, w_ref[...].astype(BF16)) + b_ref[...]


def _modulation(c, ada_w, ada_b):
    depth, d, n = ada_w.shape
    bsz = c.shape[0]
    rows = ((bsz + SUBLANES - 1) // SUBLANES) * SUBLANES
    c_pad = jnp.pad(c, ((0, rows - bsz), (0, 0)))
    tn = 1024
    out = pl.pallas_call(
        _mod_kernel,
        out_shape=jax.ShapeDtypeStruct((depth, rows, n), F32),
        grid=(depth, n // tn),
        in_specs=[
            pl.BlockSpec((rows, d), lambda l, j: (0, 0)),
            pl.BlockSpec((None, d, tn), lambda l, j: (l, 0, j)),
            pl.BlockSpec((None, 1, tn), lambda l, j: (l, 0, j)),
        ],
        out_specs=pl.BlockSpec((None, rows, tn), lambda l, j: (l, 0, j)),
        compiler_params=_cparams(("arbitrary", "arbitrary")),
        name="ada_modulation",
    )(c_pad, ada_w, ada_b.reshape(depth, 1, n))
    return out[:, :bsz]


def _inproj_kernel(x_ref, g_ref, sc_ref, sh_ref, w_ref, z_ref, kw_ref, *, tn):
    y = _rms(x_ref[...]) * g_ref[...]
    h = (y * (1.0 + sc_ref[...]) + sh_ref[...]).astype(BF16)
    for j in range(Z_WIDTH // tn):
        zj = _dot(h, w_ref[:, j * tn:(j + 1) * tn])
        z_ref[:, j * tn:(j + 1) * tn] = zj.astype(BF16)
        if j == Z_KW // tn:
            kw_ref[...] = zj[:, Z_KW - j * tn:Z_KW - j * tn + LANES]


def _in_projection(x2d, gain, mod_l, w_in_r, seq):
    t, d = x2d.shape
    tm = min(256, seq)
    per_b = seq // tm
    kern = functools.partial(_inproj_kernel, tn=1024)
    return pl.pallas_call(
        kern,
        out_shape=(jax.ShapeDtypeStruct((t, Z_WIDTH), BF16), jax.ShapeDtypeStruct((t, LANES), F32)),
        grid=(t // tm,),
        in_specs=[
            pl.BlockSpec((tm, d), lambda i: (i, 0)),
            pl.BlockSpec((1, d), lambda i: (0, 0)),
            pl.BlockSpec((None, None, 1, d), lambda i: (i // per_b, 1, 0, 0)),
            pl.BlockSpec((None, None, 1, d), lambda i: (i // per_b, 0, 0, 0)),
            pl.BlockSpec((d, Z_WIDTH), lambda i: (0, 0), pipeline_mode=pl.Buffered(1)),
        ],
        out_specs=(pl.BlockSpec((tm, Z_WIDTH), lambda i: (i, 0)), pl.BlockSpec((tm, LANES), lambda i: (i, 0))),
        compiler_params=_cparams(("arbitrary",)),
        name="norm_in_projection",
    )(x2d, gain.reshape(1, d), mod_l, mod_l, w_in_r)


def _dsa_prep_kernel(cq_ref, ckv_ref, kw_ref, gq_ref, gkv_ref, wuq_ref, wuk_ref, wqi_ref,
                     qlt_ref, qidx_ref, ckvn_ref, ckvt_ref, kidx_ref, *, tm, kc):
    qb = Q_BLOCK
    cqn = (_rms(cq_ref[...].astype(F32)) * gq_ref[...]).astype(BF16)
    ckvn = _rms(ckv_ref[...].astype(F32)) * gkv_ref[...]
    ckvn_ref[...] = ckvn.astype(BF16)
    ckvt = jnp.transpose(ckvn).astype(BF16)
    for j in range(tm // kc):
        ckvt_ref[j] = ckvt[:, j * kc:(j + 1) * kc]
    q = _dot(cqn, wuq_ref[...]).astype(BF16)
    qi = _dot(cqn, wqi_ref[...])
    qscale = ATT_HEAD_DIM ** -0.5 * LOG2E
    for h in range(N_ATT_HEADS):
        qlt = _dot_nt(wuk_ref[h], q[:, h * ATT_HEAD_DIM:(h + 1) * ATT_HEAD_DIM]) * qscale
        for j in range(tm // qb):
            qlt_ref[j, :, h * qb:(h + 1) * qb] = qlt[:, j * qb:(j + 1) * qb].astype(BF16)
    for h in range(N_IDX_HEADS):
        qidx_ref[h] = qi[:, h * LANES:(h + 1) * LANES].astype(BF16)
    kw = kw_ref[...]
    lane = lax.broadcasted_iota(I32, kw.shape, 1)
    kidx_ref[...] = jnp.where(lane < IDX_DIM, kw, 0.0).astype(BF16)


def _dsa_prep(z, kw, gq, gkv, w_uq, w_ukh, w_qidx_p, bsz, seq):
    t = z.shape[0]
    tm = min(256, seq)
    kc = min(DSA_KEY_CHUNK, seq)
    qb = Q_BLOCK
    per_b = seq // tm
    kern = functools.partial(_dsa_prep_kernel, tm=tm, kc=kc)
    return pl.pallas_call(
        kern,
        out_shape=(
            jax.ShapeDtypeStruct((bsz, seq // qb, KV_RANK, N_ATT_HEADS * qb), BF16),
            jax.ShapeDtypeStruct((bsz, N_IDX_HEADS, seq, LANES), BF16),
            jax.ShapeDtypeStruct((t, KV_RANK), BF16),
            jax.ShapeDtypeStruct((bsz, seq // kc, KV_RANK, kc), BF16),
            jax.ShapeDtypeStruct((t, LANES), BF16),
        ),
        grid=(t // tm,),
        in_specs=[
            pl.BlockSpec((tm, Q_RANK), lambda i: (i, Z_CQ // Q_RANK)),
            pl.BlockSpec((tm, KV_RANK), lambda i: (i, Z_CKV // KV_RANK)),
            pl.BlockSpec((tm, LANES), lambda i: (i, 0)),
            pl.BlockSpec((1, Q_RANK), lambda i: (0, 0)),
            pl.BlockSpec((1, KV_RANK), lambda i: (0, 0)),
            pl.BlockSpec((Q_RANK, ATT_WIDTH), lambda i: (0, 0)),
            pl.BlockSpec((N_ATT_HEADS, KV_RANK, ATT_HEAD_DIM), lambda i: (0, 0, 0)),
            pl.BlockSpec((Q_RANK, N_IDX_HEADS * LANES), lambda i: (0, 0)),
        ],
        out_specs=(
            pl.BlockSpec((None, tm // qb, KV_RANK, N_ATT_HEADS * qb), lambda i: (i // per_b, i % per_b, 0, 0)),
            pl.BlockSpec((None, N_IDX_HEADS, tm, LANES), lambda i: (i // per_b, 0, i % per_b, 0)),
            pl.BlockSpec((tm, KV_RANK), lambda i: (i, 0)),
            pl.BlockSpec((None, tm // kc, KV_RANK, kc), lambda i: (i // per_b, i % per_b, 0, 0)),
            pl.BlockSpec((tm, LANES), lambda i: (i, 0)),
        ),
        compiler_params=_cparams(("arbitrary",)),
        name="dsa_prep",
    )(z, z, kw, gq, gkv, w_uq, w_ukh, w_qidx_p)


def _dsa_attn_kernel(qidx_ref, qlt_ref, kw_ref, kidx_ref, ckvn_ref, ckvt_ref, wuvt_ref, o_ref,
                     key_ref, m_ref, l_ref, acc_ref, lo_ref, jl_ref, *, kc, k_sel, seq):
    qb = Q_BLOCK
    i = pl.program_id(1)
    nkeys = (i + 1) * qb
    npair = (nkeys + 2 * kc - 1) // (2 * kc)
    t_row = i * qb + lax.broadcasted_iota(I32, (1, qb), 1)
    sub_pos = lax.broadcasted_iota(I32, (kc, qb), 0)
    kwt = jnp.transpose(kw_ref[...]) * (N_IDX_HEADS ** -0.5 * IDX_DIM ** -0.5)

    def idx_chunk(c):
        ks = kidx_ref[pl.ds(pl.multiple_of(c * kc, kc), kc), :]
        acc = jnp.zeros((kc, qb), F32)
        for g in range(N_IDX_HEADS // 2):
            r = _dot_nt(ks, qidx_ref[2 * g:2 * g + 2].reshape(2 * qb, LANES))
            for u in range(2):
                h = 2 * g + u
                acc = acc + kwt[KW_WIDX + h:KW_WIDX + h + 1, :] * jnp.maximum(r[:, u * qb:(u + 1) * qb], 0.0)
        pos = c * kc + sub_pos
        acc = jnp.where(pos <= t_row, acc + 0.0, -jnp.inf)
        bits = pltpu.bitcast(acc, I32)
        key_ref[c] = bits ^ ((bits >> 31) & 0x7FFFFFFF)

    def pair_loop(chunk_fn):
        def body(cp, carry):
            chunk_fn(2 * cp)
            chunk_fn(2 * cp + 1)
            return carry
        lax.fori_loop(0, npair, body, 0)

    pair_loop(idx_chunk)

    nacc = 4 * SUBLANES

    def count(pred_fn):
        def body(cp, acc):
            for u in range(2):
                c = 2 * cp + u
                ind = jnp.where(pred_fn(key_ref[c], c * kc + sub_pos), 1.0, 0.0)
                acc = acc + jnp.sum(ind.reshape(kc // nacc, nacc, qb), axis=0)
            return acc
        acc = lax.fori_loop(0, npair, body, jnp.zeros((nacc, qb), F32))
        return jnp.sum(acc, axis=0, keepdims=True)

    kf = float(k_sel)

    def bit_body(it, lo):
        cand = lo + lax.shift_left(jnp.int32(1), jnp.int32(31) - it)
        cnt = count(lambda k, pos: k >= cand)
        return jnp.where(cnt >= kf, cand, lo)

    lo = lax.fori_loop(0, 32, bit_body, jnp.full((1, qb), INT_MIN, I32))
    lo_ref[...] = lo

    cnt_gt = count(lambda k, pos: k > lo)
    cnt_eq = count(lambda k, pos: k == lo)
    need = kf - cnt_gt
    neg_inf_key = INT_MIN + 0x7FFFFF
    bad = jnp.where((cnt_eq > need) & (lo > neg_inf_key), 1.0, 0.0)
    jl_ref[...] = jnp.full((1, qb), seq, I32)

    @pl.when(jnp.max(bad) > 0.0)
    def _():
        nbits = max(1, (seq - 1).bit_length())

        def jbit_body(it, v):
            cand = v + lax.shift_left(jnp.int32(1), jnp.int32(nbits - 1) - it)
            cnt = count(lambda k, pos: (k == lo) & (pos < cand))
            return jnp.where(cnt < need, cand, v)

        jl_ref[...] = lax.fori_loop(0, nbits, jbit_body, jnp.zeros((1, qb), I32))

    hg = 2 * qb
    m_ref[...] = jnp.full(m_ref.shape, NEG_BIG, F32)

    def max_chunk(c):
        kv = ckvn_ref[pl.ds(pl.multiple_of(c * kc, kc), kc), :]
        key = key_ref[c]
        pos = c * kc + sub_pos
        lo_c = lo_ref[...]
        tie = jnp.where(key == lo_c, jnp.where(pos <= jl_ref[...], 0.0, NEG_BIG), NEG_BIG)
        bias = jnp.where(pos <= t_row, jnp.where(key > lo_c, 0.0, tie), NEG_BIG)
        key_ref[c] = pltpu.bitcast(bias, I32)
        bias2 = jnp.concatenate([bias, bias], axis=1)
        for g in range(N_ATT_HEADS // 2):
            cols = slice(g * hg, (g + 1) * hg)
            s = _dot(kv, qlt_ref[:, cols]) + bias2
            m_ref[:, cols] = jnp.maximum(m_ref[:, cols], jnp.max(s, axis=0, keepdims=True))

    pair_loop(max_chunk)

    l_ref[...] = jnp.zeros(l_ref.shape, F32)
    acc_ref[...] = jnp.zeros(acc_ref.shape, F32)

    def att_chunk(c):
        kv = ckvn_ref[pl.ds(pl.multiple_of(c * kc, kc), kc), :]
        kvt = ckvt_ref[c]
        bias = pltpu.bitcast(key_ref[c], F32)
        bias2 = jnp.concatenate([bias, bias], axis=1)
        for g in range(N_ATT_HEADS // 2):
            cols = slice(g * hg, (g + 1) * hg)
            p = jnp.exp2(_dot(kv, qlt_ref[:, cols]) + bias2 - m_ref[:, cols])
            l_ref[:, cols] = l_ref[:, cols] + jnp.sum(p, axis=0, keepdims=True)
            acc_ref[:, cols] = acc_ref[:, cols] + _dot(kvt, p.astype(BF16))

    pair_loop(att_chunk)

    o_t = (acc_ref[...] * (1.0 / l_ref[...])).astype(BF16)
    for h in range(N_ATT_HEADS):
        ot_h = _dot(wuvt_ref[h], o_t[:, h * qb:(h + 1) * qb])
        o_ref[:, h * ATT_HEAD_DIM:(h + 1) * ATT_HEAD_DIM] = jnp.transpose(ot_h).astype(o_ref.dtype)


def _dsa_attention(qidx, qlt, kw, kidx, ckvn, ckvt, w_uvt, bsz, seq):
    t = kw.shape[0]
    qb = Q_BLOCK
    kc = min(DSA_KEY_CHUNK, seq)
    nb = seq // qb
    k_sel = min(TOPK_MAX, seq // 4)
    assert seq % (2 * kc) == 0
    kern = functools.partial(_dsa_attn_kernel, kc=kc, k_sel=k_sel, seq=seq)
    return pl.pallas_call(
        kern,
        out_shape=jax.ShapeDtypeStruct((t, ATT_WIDTH), BF16),
        grid=(bsz, nb),
        in_specs=[
            pl.BlockSpec((None, N_IDX_HEADS, qb, LANES), lambda b, i: (b, 0, i, 0)),
            pl.BlockSpec((None, None, KV_RANK, N_ATT_HEADS * qb), lambda b, i: (b, i, 0, 0)),
            pl.BlockSpec((qb, LANES), lambda b, i: (b * nb + i, 0)),
            pl.BlockSpec((seq, LANES), lambda b, i: (b, 0)),
            pl.BlockSpec((seq, KV_RANK), lambda b, i: (b, 0)),
            pl.BlockSpec((None, seq // kc, KV_RANK, kc), lambda b, i: (b, 0, 0, 0)),
            pl.BlockSpec((N_ATT_HEADS, ATT_HEAD_DIM, KV_RANK), lambda b, i: (0, 0, 0)),
        ],
        out_specs=pl.BlockSpec((qb, ATT_WIDTH), lambda b, i: (b * nb + i, 0)),
        scratch_shapes=[
            pltpu.VMEM((seq // kc, kc, qb), I32),
            pltpu.VMEM((1, N_ATT_HEADS * qb), F32),
            pltpu.VMEM((1, N_ATT_HEADS * qb), F32),
            pltpu.VMEM((KV_RANK, N_ATT_HEADS * qb), F32),
            pltpu.VMEM((1, qb), I32),
            pltpu.VMEM((1, qb), I32),
        ],
        compiler_params=_cparams(("arbitrary", "arbitrary")),
        name="dsa_attention",
    )(qidx, qlt, kw, kidx, ckvn, ckvt, w_uvt)


def _log_sigmoid(x):
    return jnp.minimum(x, 0.0) - jnp.log(1.0 + jnp.exp(-jnp.abs(x)))


def _mlstm_kernel(q_ref, k_ref, v_ref, og_ref, kw_ref, gb_ref, gn_ref, out_ref,
                  c_ref, n_ref, m_ref, *, chunk):
    lc = chunk
    dk, dv = MLSTM_QK_DIM, MLSTM_V_DIM

    @pl.when(pl.program_id(1) == 0)
    def _():
        c_ref[...] = jnp.zeros(c_ref.shape, F32)
        n_ref[...] = jnp.zeros(n_ref.shape, F32)
        m_ref[...] = jnp.zeros(m_ref.shape, F32)

    kwb = kw_ref[...] + gb_ref[...]
    kwt = jnp.transpose(kwb)
    row = lax.broadcasted_iota(I32, (lc, lc), 0)
    col = lax.broadcasted_iota(I32, (lc, lc), 1)
    causal = col <= row
    tri = jnp.where(causal, 1.0, 0.0)
    tri_t = jnp.where(row <= col, 1.0, 0.0)
    cum_col = _dot_f32(tri, _log_sigmoid(kwb))
    cum_row = _dot_f32(_log_sigmoid(kwt), tri_t)

    for h in range(N_MLSTM_HEADS):
        i_col = kwb[:, KW_MI + h:KW_MI + h + 1]
        i_row = kwt[KW_MI + h:KW_MI + h + 1, :]
        b_col = cum_col[:, KW_MF + h:KW_MF + h + 1]
        b_row = cum_row[KW_MF + h:KW_MF + h + 1, :]
        m_prev = m_ref[h]

        dmat = jnp.where(causal, b_col - b_row + i_row, -jnp.inf)
        m_inter = b_col + m_prev
        m_j = jnp.maximum(m_inter, jnp.max(dmat, axis=1, keepdims=True))
        w_inter = jnp.exp(m_inter - m_j)

        qh = q_ref[:, h * dk:(h + 1) * dk].astype(F32) * (dk ** -0.5)
        kh = k_ref[:, h * dk:(h + 1) * dk].astype(F32)
        vb = v_ref[:, h * dv:(h + 1) * dv]
        qb = qh.astype(BF16)
        sc = _dot_nt(qb, kh.astype(BF16)) * jnp.exp(dmat - m_j)
        c_old = c_ref[h]
        n_old = n_ref[h]
        num = w_inter * _dot(qb, c_old.astype(BF16)) + _dot(sc.astype(BF16), vb)
        den = (w_inter * jnp.sum(qh * n_old, axis=1, keepdims=True)
               + jnp.sum(sc, axis=1, keepdims=True))
        hval = num / jnp.maximum(jnp.abs(den), jnp.exp(-m_j))

        hn = _rms(hval) * gn_ref[:, h * dv:(h + 1) * dv]
        gate = jax.nn.sigmoid(og_ref[:, h * dv:(h + 1) * dv].astype(F32))
        out_ref[:, h * dv:(h + 1) * dv] = (hn * gate).astype(out_ref.dtype)

        b_last = b_col[lc - 1:lc, :]
        g_col = b_last - b_col + i_col
        m_new = jnp.maximum(b_last + m_prev, jnp.max(g_col, axis=0, keepdims=True))
        w_old = jnp.exp(b_last + m_prev - m_new)
        kwgt = kh * jnp.exp(g_col - m_new)
        c_ref[h] = w_old * c_old + _dot(jnp.transpose(kwgt).astype(BF16), vb)
        n_ref[h] = w_old * n_old + jnp.sum(kwgt, axis=0, keepdims=True)
        m_ref[h] = m_new


def _mlstm(z, kw, gate_bias, norm_g, bsz, seq):
    t = z.shape[0]
    lc = min(256, seq)
    nc = seq // lc
    qk_w = N_MLSTM_HEADS * MLSTM_QK_DIM
    kern = functools.partial(_mlstm_kernel, chunk=lc)
    row = lambda b, c: b * nc + c
    return pl.pallas_call(
        kern,
        out_shape=jax.ShapeDtypeStruct((t, MLSTM_WIDTH), BF16),
        grid=(bsz, nc),
        in_specs=[
            pl.BlockSpec((lc, qk_w), lambda b, c: (row(b, c), Z_MQ // qk_w)),
            pl.BlockSpec((lc, qk_w), lambda b, c: (row(b, c), Z_MK // qk_w)),
            pl.BlockSpec((lc, MLSTM_WIDTH), lambda b, c: (row(b, c), Z_MV // MLSTM_WIDTH)),
            pl.BlockSpec((lc, MLSTM_WIDTH), lambda b, c: (row(b, c), Z_MO // MLSTM_WIDTH)),
            pl.BlockSpec((lc, LANES), lambda b, c: (row(b, c), 0)),
            pl.BlockSpec((1, LANES), lambda b, c: (0, 0)),
            pl.BlockSpec((1, MLSTM_WIDTH), lambda b, c: (0, 0)),
        ],
        out_specs=pl.BlockSpec((lc, MLSTM_WIDTH), lambda b, c: (row(b, c), 0)),
        scratch_shapes=[
            pltpu.VMEM((N_MLSTM_HEADS, MLSTM_QK_DIM, MLSTM_V_DIM), F32),
            pltpu.VMEM((N_MLSTM_HEADS, 1, MLSTM_QK_DIM), F32),
            pltpu.VMEM((N_MLSTM_HEADS, 1, 1), F32),
        ],
        compiler_params=_cparams(("arbitrary", "arbitrary")),
        name="mlstm",
    )(z, z, z, z, kw, gate_bias, norm_g)


def _outproj_router_kernel(att_ref, hm_ref, x_ref, g1_ref, gn_ref, sc_ref, sh_ref, wo_ref, rwa_ref, rb_ref,
                           x1_ref, h2_ref, ri_ref, rf_ref, cnt_ref, carry_ref, *, tm):
    @pl.when(pl.program_id(0) == 0)
    def _():
        carry_ref[...] = jnp.zeros(carry_ref.shape, F32)

    y = _dot(att_ref[...], wo_ref[:ATT_WIDTH, :]) + _dot(hm_ref[...], wo_ref[ATT_WIDTH:, :])
    x1 = x_ref[...] + g1_ref[...] * y
    x1_ref[...] = x1
    h2 = _rms(x1) * gn_ref[...] * (1.0 + sc_ref[...]) + sh_ref[...]
    h2_ref[...] = h2

    h2_hi = h2.astype(BF16)
    h2_lo = (h2 - h2_hi.astype(F32)).astype(BF16)
    both = _dot(h2_hi, rwa_ref[...])
    logits = both[:, :LANES] + both[:, LANES:] + _dot(h2_lo, rwa_ref[:, :LANES])
    aff = jax.nn.sigmoid(jnp.transpose(logits)[:N_EXPERTS, :])
    sel = aff + rb_ref[...]

    epg = EXPERTS_PER_GROUP
    riota = lax.broadcasted_iota(I32, (epg, tm), 0)
    best = None
    for g in range(N_EXPERT_GROUPS):
        v = sel[g * epg:(g + 1) * epg]
        a = aff[g * epg:(g + 1) * epg]
        m1 = jnp.max(v, axis=0, keepdims=True)
        i1 = jnp.min(jnp.where(v == m1, riota, epg), axis=0, keepdims=True)
        v2 = jnp.where(riota == i1, -jnp.inf, v)
        m2 = jnp.max(v2, axis=0, keepdims=True)
        i2 = jnp.min(jnp.where(v2 == m2, riota, epg), axis=0, keepdims=True)
        a1 = jnp.sum(jnp.where(riota == i1, a, 0.0), axis=0, keepdims=True)
        a2 = jnp.sum(jnp.where(riota == i2, a, 0.0), axis=0, keepdims=True)
        cur = (m1 + m2, i1 + g * epg, i2 + g * epg, a1, a2)
        if best is None:
            best = cur
        else:
            better = cur[0] > best[0]
            best = tuple(jnp.where(better, c_, b_) for c_, b_ in zip(cur, best))
    _, e1, e2, a1, a2 = best
    asum = a1 + a2

    eiota = lax.broadcasted_iota(I32, (N_EXPERTS, tm), 0)
    hit1 = eiota == e1
    hit2 = eiota == e2
    onehot = jnp.where(hit1, 1.0, jnp.where(hit2, 1.0, 0.0))
    srow = lax.broadcasted_iota(I32, (tm, tm), 0)
    scol = lax.broadcasted_iota(I32, (tm, tm), 1)
    before = jnp.where(srow < scol, 1.0, 0.0).astype(BF16)
    rank = _dot(onehot.astype(BF16), before) + carry_ref[...]
    r1 = jnp.sum(jnp.where(hit1, rank, 0.0), axis=0, keepdims=True)
    r2 = jnp.sum(jnp.where(hit2, rank, 0.0), axis=0, keepdims=True)
    carry = carry_ref[...] + jnp.sum(onehot, axis=1, keepdims=True)
    carry_ref[...] = carry
    cnt_ref[...] = jnp.broadcast_to(carry, cnt_ref.shape)

    zi = jnp.zeros((1, tm), I32)
    ri_ref[...] = jnp.concatenate(
        [e1, e2, r1.astype(I32), r2.astype(I32), zi, zi, zi, zi], axis=0)
    zf = jnp.zeros((1, tm), F32)
    rf_ref[...] = jnp.concatenate([a1 / asum, a2 / asum, zf, zf, zf, zf, zf, zf], axis=0)


def _outproj_router(att, hm, x2d, mod_l, gain, w_out_b, router_w, router_b, seq):
    t, d = x2d.shape
    tm = min(256, seq)
    per_b = seq // tm
    nt = t // tm
    kern = functools.partial(_outproj_router_kernel, tm=tm)
    rw = jnp.pad(router_w, ((0, 0), (0, LANES - N_EXPERTS)))
    rw_hi = rw.astype(BF16)
    rw_lo = (rw - rw_hi.astype(F32)).astype(BF16)
    rw_split = jnp.concatenate([rw_hi, rw_lo], axis=1)
    mod_spec = lambda which: pl.BlockSpec((None, None, 1, d), lambda i: (i // per_b, which, 0, 0))
    return pl.pallas_call(
        kern,
        out_shape=(
            jax.ShapeDtypeStruct((t, d), F32),
            jax.ShapeDtypeStruct((t, d), F32),
            jax.ShapeDtypeStruct((nt, SUBLANES, tm), I32),
            jax.ShapeDtypeStruct((nt, SUBLANES, tm), F32),
            jax.ShapeDtypeStruct((N_EXPERTS, LANES), F32),
        ),
        grid=(nt,),
        in_specs=[
            pl.BlockSpec((tm, ATT_WIDTH), lambda i: (i, 0)),
            pl.BlockSpec((tm, MLSTM_WIDTH), lambda i: (i, 0)),
            pl.BlockSpec((tm, d), lambda i: (i, 0)),
            mod_spec(2),
            pl.BlockSpec((1, d), lambda i: (0, 0)),
            mod_spec(4),
            mod_spec(3),
            pl.BlockSpec((ATT_WIDTH + MLSTM_WIDTH, d), lambda i: (0, 0)),
            pl.BlockSpec((d, 2 * LANES), lambda i: (0, 0)),
            pl.BlockSpec((N_EXPERTS, 1), lambda i: (0, 0)),
        ],
        out_specs=(
            pl.BlockSpec((tm, d), lambda i: (i, 0)),
            pl.BlockSpec((tm, d), lambda i: (i, 0)),
            pl.BlockSpec((None, SUBLANES, tm), lambda i: (i, 0, 0)),
            pl.BlockSpec((None, SUBLANES, tm), lambda i: (i, 0, 0)),
            pl.BlockSpec((N_EXPERTS, LANES), lambda i: (0, 0)),
        ),
        scratch_shapes=[pltpu.VMEM((N_EXPERTS, 1), F32)],
        compiler_params=_cparams(("arbitrary",)),
        name="outproj_router",
    )(att, hm, x2d, mod_l, gain.reshape(1, d), mod_l, mod_l, w_out_b, rw_split, router_b.reshape(N_EXPERTS, 1))


def _dispatch_kernel(pos_ref, src_ref, dst_ref, sem, *, tc):
    i = pl.program_id(0)

    def row_copy(r, slot):
        p = pos_ref[2 * (i * tc + r) + slot]
        return pltpu.make_async_copy(src_ref.at[pl.ds(r, 1)], dst_ref.at[pl.ds(p, 1)], sem)

    def start_body(r, carry):
        row_copy(r, 0).start()
        row_copy(r, 1).start()
        return carry

    lax.fori_loop(0, tc, start_body, 0, unroll=8)

    def wait_body(r, carry):
        row_copy(r, 0).wait()
        row_copy(r, 1).wait()
        return carry

    lax.fori_loop(0, tc, wait_body, 0, unroll=8)


def _dispatch(pos, h2, seq):
    t, d = h2.shape
    tc = min(256, seq)
    kern = functools.partial(_dispatch_kernel, tc=tc)
    return pl.pallas_call(
        kern,
        out_shape=jax.ShapeDtypeStruct((2 * t, d), h2.dtype),
        grid_spec=pltpu.PrefetchScalarGridSpec(
            num_scalar_prefetch=1,
            grid=(t // tc,),
            in_specs=[pl.BlockSpec((tc, d), lambda i, pos: (i, 0))],
            out_specs=pl.BlockSpec(memory_space=pl.ANY),
            scratch_shapes=[pltpu.SemaphoreType.DMA],
        ),
        compiler_params=pltpu.CompilerParams(dimension_semantics=("arbitrary",), has_side_effects=True,
                                             vmem_limit_bytes=VMEM_LIMIT),
        name="moe_dispatch",
    )(pos, h2)


def _gmm_kernel(vt_ref, ve_ref, vfirst_ref, vnew_ref, vstart_ref, vend_ref, xs_ref, wgu_ref, wd_ref, ys_ref,
                wgu_b, wd_b, *, tm):
    v = pl.program_id(0)
    start = vstart_ref[v]
    end = vend_ref[v]

    @pl.when(vnew_ref[v] == 1)
    def _():
        wgu_b[...] = wgu_ref[...].astype(BF16)
        wd_b[...] = wd_ref[...].astype(BF16)

    @pl.when(end > start)
    def _():
        x = xs_ref[...].astype(BF16)
        gu = _dot(x, wgu_b[...])
        gt = gu[:, :D_FF_EXPERT]
        up = gu[:, D_FF_EXPERT:]
        act = (gt * jax.nn.sigmoid(gt)) * up
        y = _dot(act.astype(BF16), wd_b[...])
        rows = vt_ref[v] * tm + lax.broadcasted_iota(I32, (tm, 1), 0)
        mine = (rows >= start) & (rows < end)

        @pl.when(vfirst_ref[v] == 1)
        def _():
            ys_ref[...] = jnp.where(mine, y, 0.0)

        @pl.when(vfirst_ref[v] == 0)
        def _():
            ys_ref[...] = jnp.where(mine, y, ys_ref[...])


def _gmm(sched, xs, w_gate_up, w_down, layer, tm):
    rows, d = xs.shape
    n_visits = sched[0].shape[0]
    kern = functools.partial(_gmm_kernel, tm=tm)
    return pl.pallas_call(
        kern,
        out_shape=jax.ShapeDtypeStruct((rows, d), F32),
        grid_spec=pltpu.PrefetchScalarGridSpec(
            num_scalar_prefetch=6,
            grid=(n_visits,),
            in_specs=[
                pl.BlockSpec((tm, d), lambda v, vt, ve, *_: (vt[v], 0)),
                pl.BlockSpec((None, None, d, 2 * D_FF_EXPERT), lambda v, vt, ve, *_: (layer, ve[v], 0, 0)),
                pl.BlockSpec((None, None, D_FF_EXPERT, d), lambda v, vt, ve, *_: (layer, ve[v], 0, 0)),
            ],
            out_specs=pl.BlockSpec((tm, d), lambda v, vt, ve, *_: (vt[v], 0)),
            scratch_shapes=[pltpu.VMEM((d, 2 * D_FF_EXPERT), BF16), pltpu.VMEM((D_FF_EXPERT, d), BF16)],
        ),
        compiler_params=_cparams(("arbitrary",)),
        name="moe_experts",
    )(*sched, xs, w_gate_up, w_down)


def _visit_schedule(counts, n_rows, tm):
    ends = jnp.cumsum(counts)
    starts = ends - counts
    n_tiles = n_rows // tm
    n_visits = n_tiles + N_EXPERTS - 1
    first_tile = starts // tm
    last_tile = jnp.maximum(ends - 1, 0) // tm
    nv = jnp.where(counts > 0, last_tile - first_tile + 1, 0)
    vend_cum = jnp.cumsum(nv)
    voff = vend_cum - nv
    total = vend_cum[-1]
    v = jnp.arange(n_visits, dtype=I32)
    live = v < total
    e = jnp.searchsorted(vend_cum, jnp.minimum(v, total - 1), side="right").astype(I32)
    e = jnp.minimum(e, N_EXPERTS - 1)
    tile = jnp.where(live, first_tile[e] + (v - voff[e]), n_tiles - 1).astype(I32)
    prev_tile = jnp.concatenate([jnp.full((1,), -1, I32), tile[:-1]])
    first = jnp.where(live & (tile != prev_tile), 1, 0).astype(I32)
    prev_e = jnp.concatenate([jnp.full((1,), -1, I32), e[:-1]])
    new_e = jnp.where(live & (e != prev_e), 1, 0).astype(I32)
    vstart = jnp.where(live, starts[e], 0).astype(I32)
    vend = jnp.where(live, ends[e], 0).astype(I32)
    return (tile, e, first, new_e, vstart, vend), starts


def _combine_kernel(pos_ref, ys_ref, x1_ref, ga_ref, gb_ref, g2_ref, fg_ref, o_ref, buf_ref, sem, *, tc, final):
    i = pl.program_id(0)

    def row_copy(r, slot):
        p = pos_ref[2 * (i * tc + r) + slot]
        return pltpu.make_async_copy(ys_ref.at[pl.ds(p, 1)], buf_ref.at[slot, pl.ds(r, 1)], sem)

    def start_body(r, carry):
        row_copy(r, 0).start()
        row_copy(r, 1).start()
        return carry

    lax.fori_loop(0, tc, start_body, 0, unroll=8)

    def wait_body(r, carry):
        row_copy(r, 0).wait()
        row_copy(r, 1).wait()
        return carry

    lax.fori_loop(0, tc, wait_body, 0, unroll=8)

    moe = buf_ref[0] * ga_ref[...] + buf_ref[1] * gb_ref[...]
    x2 = x1_ref[...] + g2_ref[...] * moe
    if final:
        x2 = _rms(x2) * fg_ref[...]
    o_ref[...] = x2


def _combine(pos, ys, x1, gate_a, gate_b, mod_l, final_g, seq, final):
    t, d = x1.shape
    tc = min(256, seq)
    per_b = seq // tc
    kern = functools.partial(_combine_kernel, tc=tc, final=final)
    return pl.pallas_call(
        kern,
        out_shape=jax.ShapeDtypeStruct((t, d), F32),
        grid_spec=pltpu.PrefetchScalarGridSpec(
            num_scalar_prefetch=1,
            grid=(t // tc,),
            in_specs=[
                pl.BlockSpec(memory_space=pl.ANY),
                pl.BlockSpec((tc, d), lambda i, pos: (i, 0)),
                pl.BlockSpec((tc, 1), lambda i, pos: (i, 0)),
                pl.BlockSpec((tc, 1), lambda i, pos: (i, 0)),
                pl.BlockSpec((None, None, 1, d), lambda i, pos: (i // per_b, 5, 0, 0)),
                pl.BlockSpec((1, d), lambda i, pos: (0, 0)),
            ],
            out_specs=pl.BlockSpec((tc, d), lambda i, pos: (i, 0)),
            scratch_shapes=[pltpu.VMEM((2, tc, d), F32), pltpu.SemaphoreType.DMA],
        ),
        compiler_params=_cparams(("arbitrary",)),
        name="moe_combine",
    )(pos, ys, x1, gate_a, gate_b, mod_l, final_g.reshape(1, d))


def _rearrange_w_in(w):
    o_kidx = Q_RANK + KV_RANK
    o_mq = o_kidx + IDX_DIM + N_IDX_HEADS
    o_mi = o_mq + 2 * N_MLSTM_HEADS * MLSTM_QK_DIM + 2 * MLSTM_WIDTH
    gates = w[:, o_mi:o_mi + 2 * N_MLSTM_HEADS]
    small = jnp.concatenate([w[:, o_kidx:o_mq], gates], axis=1)
    small = jnp.pad(small, ((0, 0), (0, Z_MQ - Z_KW - small.shape[1])))
    return jnp.concatenate([w[:, :o_kidx], small, w[:, o_mq:o_mi]], axis=1).astype(BF16)


def kernel(x, c, ada_w, ada_b, mix_norm_g, w_in, cq_norm_g, ckv_norm_g, w_uq, w_uk, w_uv, w_qidx,
           mlstm_i_b, mlstm_f_b, mlstm_norm_g, w_out, ffn_norm_g, router_w, router_b, w_gate_up,
           w_down, final_norm_g):
    bsz, seq, d = x.shape
    depth = ada_w.shape[0]
    t = bsz * seq
    assert seq % Q_BLOCK == 0 and seq % min(512, seq) == 0

    mod = _modulation(c, ada_w, ada_b).reshape(depth, bsz, 6, 1, d)
    gmm_tm = 256

    x2d = x.reshape(t, d)
    for l in range(depth):
        mod_l = mod[l]
        z, kw = _in_projection(x2d, mix_norm_g[l], mod_l, _rearrange_w_in(w_in[l]), seq)
        w_ukh = jnp.transpose(w_uk[l], (1, 0, 2)).astype(BF16)
        w_uvt = jnp.transpose(w_uv[l], (1, 2, 0)).astype(BF16)
        w_qidx_p = jnp.pad(w_qidx[l].reshape(Q_RANK, N_IDX_HEADS, IDX_DIM),
                           ((0, 0), (0, 0), (0, LANES - IDX_DIM))).reshape(Q_RANK, N_IDX_HEADS * LANES).astype(BF16)
        qlt, qidx, ckvn, ckvt, kidx = _dsa_prep(z, kw, cq_norm_g[l].reshape(1, Q_RANK),
                                                ckv_norm_g[l].reshape(1, KV_RANK),
                                                w_uq[l].astype(BF16), w_ukh, w_qidx_p, bsz, seq)
        att = _dsa_attention(qidx, qlt, kw, kidx, ckvn, ckvt, w_uvt, bsz, seq)
        gate_bias = jnp.zeros((1, LANES), F32)
        gate_bias = gate_bias.at[0, KW_MI:KW_MI + N_MLSTM_HEADS].set(mlstm_i_b[l])
        gate_bias = gate_bias.at[0, KW_MF:KW_MF + N_MLSTM_HEADS].set(mlstm_f_b[l])
        hm = _mlstm(z, kw, gate_bias, mlstm_norm_g[l].reshape(1, MLSTM_WIDTH), bsz, seq)
        x1, h2, ri, rf, cnt = _outproj_router(att, hm, x2d, mod_l, ffn_norm_g[l], w_out[l].astype(BF16),
                                              router_w, router_b, seq)
        counts = cnt[:, 0].astype(I32)
        sched, starts = _visit_schedule(counts, 2 * t, gmm_tm)
        e1 = ri[:, 0, :].reshape(t)
        e2 = ri[:, 1, :].reshape(t)
        pos = jnp.stack([starts[e1] + ri[:, 2, :].reshape(t), starts[e2] + ri[:, 3, :].reshape(t)],
                        axis=-1).reshape(2 * t).astype(I32)
        xs = _dispatch(pos, h2, seq)
        ys = _gmm(sched, xs, w_gate_up, w_down, l, gmm_tm)
        x2d = _combine(pos, ys, x1, rf[:, 0, :].reshape(t, 1), rf[:, 1, :].reshape(t, 1), mod_l,
                       final_norm_g, seq, final=(l == depth - 1))
    return x2d.reshape(bsz, seq, d)
```

```python
import functools

import jax
import jax.numpy as jnp
from jax import lax
from jax.experimental import pallas as pl
from jax.experimental.pallas import tpu as pltpu

F32 = jnp.float32
BF16 = jnp.bfloat16
I32 = jnp.int32
I16 = jnp.int16

EPS = 1e-6

N_ATT_HEADS = 8
ATT_HEAD_DIM = 128
Q_RANK = 512
KV_RANK = 256
N_IDX_HEADS = 8
IDX_DIM = 64
TOPK_MAX = 256
Q_BLOCK = 128
N_MLSTM_HEADS = 4
MLSTM_QK_DIM = 128
MLSTM_V_DIM = 256
N_EXPERT_GROUPS = 4
EXPERTS_PER_GROUP = 8
N_EXPERTS = N_EXPERT_GROUPS * EXPERTS_PER_GROUP
D_FF_EXPERT = 512
ATT_WIDTH = N_ATT_HEADS * ATT_HEAD_DIM
MLSTM_WIDTH = N_MLSTM_HEADS * MLSTM_V_DIM

LANES = 128
SUBLANES = 8
VMEM_LIMIT = 52 * 1024 * 1024

Z_CQ = 0
Z_CKV = 512
Z_KW = 768
Z_MQ = 1024
Z_MK = 1536
Z_MV = 2048
Z_MO = 3072
Z_WIDTH = 4096
KW_WIDX = 64
KW_MI = 72
KW_MF = 76

NEG_BIG = -1e30
LOG2E = 1.4426950408889634
DSA_KEY_CHUNK = 256
INT_MIN = -2147483648


def _cparams(sem):
    return pltpu.CompilerParams(dimension_semantics=sem, vmem_limit_bytes=VMEM_LIMIT)


def _rms(x):
    return x * lax.rsqrt(jnp.mean(x * x, axis=-1, keepdims=True) + EPS)


def _dot(a, b):
    return jnp.dot(a, b, preferred_element_type=F32)


def _dot_nt(a, b):
    return lax.dot_general(a, b, (((1,), (1,)), ((), ())), preferred_element_type=F32)


def _dot_f32(a, b):
    return jnp.dot(a, b, preferred_element_type=F32, precision=lax.Precision.HIGHEST)


def _mod_kernel(c_ref, w_ref, b_ref, o_ref):
    c = c_ref[...]
    ca = (c * jax.nn.sigmoid(c)).astype(BF16)
    o_ref[...] = _dot(ca, w_ref[...].astype(BF16)) + b_ref[...]


def _modulation(c, ada_w, ada_b):
    depth, d, n = ada_w.shape
    bsz = c.shape[0]
    rows = ((bsz + SUBLANES - 1) // SUBLANES) * SUBLANES
    c_pad = jnp.pad(c, ((0, rows - bsz), (0, 0)))
    tn = 1024
    out = pl.pallas_call(
        _mod_kernel,
        out_shape=jax.ShapeDtypeStruct((depth, rows, n), F32),
        grid=(depth, n // tn),
        in_specs=[
            pl.BlockSpec((rows, d), lambda l, j: (0, 0)),
            pl.BlockSpec((None, d, tn), lambda l, j: (l, 0, j)),
            pl.BlockSpec((None, 1, tn), lambda l, j: (l, 0, j)),
        ],
        out_specs=pl.BlockSpec((None, rows, tn), lambda l, j: (l, 0, j)),
        compiler_params=_cparams(("arbitrary", "arbitrary")),
        name="ada_modulation",
    )(c_pad, ada_w, ada_b.reshape(depth, 1, n))
    return out[:, :bsz]


def _inproj_kernel(x_ref, g_ref, sc_ref, sh_ref, w_ref, z_ref, kw_ref, *, tn):
    y = _rms(x_ref[...]) * g_ref[...]
    h = (y * (1.0 + sc_ref[...]) + sh_ref[...]).astype(BF16)
    for j in range(Z_WIDTH // tn):
        zj = _dot(h, w_ref[:, j * tn:(j + 1) * tn])
        z_ref[:, j * tn:(j + 1) * tn] = zj.astype(BF16)
        if j == Z_KW // tn:
            kw_ref[...] = zj[:, Z_KW - j * tn:Z_KW - j * tn + LANES]


def _in_projection(x2d, gain, mod_l, w_in_r, seq):
    t, d = x2d.shape
    tm = min(256, seq)
    per_b = seq // tm
    kern = functools.partial(_inproj_kernel, tn=1024)
    return pl.pallas_call(
        kern,
        out_shape=(jax.ShapeDtypeStruct((t, Z_WIDTH), BF16), jax.ShapeDtypeStruct((t, LANES), F32)),
        grid=(t // tm,),
        in_specs=[
            pl.BlockSpec((tm, d), lambda i: (i, 0)),
            pl.BlockSpec((1, d), lambda i: (0, 0)),
            pl.BlockSpec((None, None, 1, d), lambda i: (i // per_b, 1, 0, 0)),
            pl.BlockSpec((None, None, 1, d), lambda i: (i // per_b, 0, 0, 0)),
            pl.BlockSpec((d, Z_WIDTH), lambda i: (0, 0), pipeline_mode=pl.Buffered(1)),
        ],
        out_specs=(pl.BlockSpec((tm, Z_WIDTH), lambda i: (i, 0)), pl.BlockSpec((tm, LANES), lambda i: (i, 0))),
        compiler_params=_cparams(("arbitrary",)),
        name="norm_in_projection",
    )(x2d, gain.reshape(1, d), mod_l, mod_l, w_in_r)


def _dsa_prep_kernel(cq_ref, ckv_ref, kw_ref, gq_ref, gkv_ref, wuq_ref, wuk_ref, wqi_ref,
                     qlt_ref, qidx_ref, ckvn_ref, ckvt_ref, kidx_ref, *, tm, kc):
    qb = Q_BLOCK
    cqn = (_rms(cq_ref[...].astype(F32)) * gq_ref[...]).astype(BF16)
    ckvn = _rms(ckv_ref[...].astype(F32)) * gkv_ref[...]
    ckvn_ref[...] = ckvn.astype(BF16)
    ckvt = jnp.transpose(ckvn).astype(BF16)
    for j in range(tm // kc):
        ckvt_ref[j] = ckvt[:, j * kc:(j + 1) * kc]
    q = _dot(cqn, wuq_ref[...]).astype(BF16)
    qi = _dot(cqn, wqi_ref[...])
    qscale = ATT_HEAD_DIM ** -0.5 * LOG2E
    for h in range(N_ATT_HEADS):
        qlt = _dot_nt(wuk_ref[h], q[:, h * ATT_HEAD_DIM:(h + 1) * ATT_HEAD_DIM]) * qscale
        for j in range(tm // qb):
            qlt_ref[j, :, h * qb:(h + 1) * qb] = qlt[:, j * qb:(j + 1) * qb].astype(BF16)
    for h in range(N_IDX_HEADS):
        qidx_ref[h] = qi[:, h * LANES:(h + 1) * LANES].astype(BF16)
    kw = kw_ref[...]
    lane = lax.broadcasted_iota(I32, kw.shape, 1)
    kidx_ref[...] = jnp.where(lane < IDX_DIM, kw, 0.0).astype(BF16)


def _dsa_prep(z, kw, gq, gkv, w_uq, w_ukh, w_qidx_p, bsz, seq):
    t = z.shape[0]
    tm = min(256, seq)
    kc = min(DSA_KEY_CHUNK, seq)
    qb = Q_BLOCK
    per_b = seq // tm
    kern = functools.partial(_dsa_prep_kernel, tm=tm, kc=kc)
    return pl.pallas_call(
        kern,
        out_shape=(
            jax.ShapeDtypeStruct((bsz, seq // qb, KV_RANK, N_ATT_HEADS * qb), BF16),
            jax.ShapeDtypeStruct((bsz, N_IDX_HEADS, seq, LANES), BF16),
            jax.ShapeDtypeStruct((t, KV_RANK), BF16),
            jax.ShapeDtypeStruct((bsz, seq // kc, KV_RANK, kc), BF16),
            jax.ShapeDtypeStruct((t, LANES), BF16),
        ),
        grid=(t // tm,),
        in_specs=[
            pl.BlockSpec((tm, Q_RANK), lambda i: (i, Z_CQ // Q_RANK)),
            pl.BlockSpec((tm, KV_RANK), lambda i: (i, Z_CKV // KV_RANK)),
            pl.BlockSpec((tm, LANES), lambda i: (i, 0)),
            pl.BlockSpec((1, Q_RANK), lambda i: (0, 0)),
            pl.BlockSpec((1, KV_RANK), lambda i: (0, 0)),
            pl.BlockSpec((Q_RANK, ATT_WIDTH), lambda i: (0, 0)),
            pl.BlockSpec((N_ATT_HEADS, KV_RANK, ATT_HEAD_DIM), lambda i: (0, 0, 0)),
            pl.BlockSpec((Q_RANK, N_IDX_HEADS * LANES), lambda i: (0, 0)),
        ],
        out_specs=(
            pl.BlockSpec((None, tm // qb, KV_RANK, N_ATT_HEADS * qb), lambda i: (i // per_b, i % per_b, 0, 0)),
            pl.BlockSpec((None, N_IDX_HEADS, tm, LANES), lambda i: (i // per_b, 0, i % per_b, 0)),
            pl.BlockSpec((tm, KV_RANK), lambda i: (i, 0)),
            pl.BlockSpec((None, tm // kc, KV_RANK, kc), lambda i: (i // per_b, i % per_b, 0, 0)),
            pl.BlockSpec((tm, LANES), lambda i: (i, 0)),
        ),
        compiler_params=_cparams(("arbitrary",)),
        name="dsa_prep",
    )(z, z, kw, gq, gkv, w_uq, w_ukh, w_qidx_p)


def _dsa_attn_kernel(qidx_ref, qlt_ref, kw_ref, kidx_ref, ckvn_ref, ckvt_ref, wuvt_ref, o_ref,
                     key_ref, half_ref, m_ref, l_ref, acc_ref, lo_ref, jl_ref, *, kc, k_sel, seq):
    qb = Q_BLOCK
    i = pl.program_id(1)
    nkeys = (i + 1) * qb
    npair = (nkeys + 2 * kc - 1) // (2 * kc)
    t_row = i * qb + lax.broadcasted_iota(I32, (1, qb), 1)
    sub_pos = lax.broadcasted_iota(I32, (kc, qb), 0)
    kwt = jnp.transpose(kw_ref[...]) * (N_IDX_HEADS ** -0.5 * IDX_DIM ** -0.5)

    def idx_chunk(c):
        ks = kidx_ref[pl.ds(pl.multiple_of(c * kc, kc), kc), :]
        acc = jnp.zeros((kc, qb), F32)
        for g in range(N_IDX_HEADS // 2):
            r = _dot_nt(ks, qidx_ref[2 * g:2 * g + 2].reshape(2 * qb, LANES))
            for u in range(2):
                h = 2 * g + u
                acc = acc + kwt[KW_WIDX + h:KW_WIDX + h + 1, :] * jnp.maximum(r[:, u * qb:(u + 1) * qb], 0.0)
        pos = c * kc + sub_pos
        acc = jnp.where(pos <= t_row, acc + 0.0, -jnp.inf)
        bits = pltpu.bitcast(acc, I32)
        key = bits ^ ((bits >> 31) & 0x7FFFFFFF)
        key_ref[c] = key
        half_ref[c] = (key >> 16).astype(I16)

    nquad = npair // 2

    def pair_loop(chunk_fn):
        def body(cq, carry):
            for u in range(4):
                chunk_fn(4 * cq + u)
            return carry
        lax.fori_loop(0, nquad, body, 0)

        @pl.when(npair % 2 == 1)
        def _():
            chunk_fn(4 * nquad)
            chunk_fn(4 * nquad + 1)

    pair_loop(idx_chunk)

    nacc = 4 * SUBLANES

    def count(pred_fn):
        def body(cp, acc):
            for u in range(2):
                c = 2 * cp + u
                ind = jnp.where(pred_fn(key_ref[c], c * kc + sub_pos), 1.0, 0.0)
                acc = acc + jnp.sum(ind.reshape(kc // nacc, nacc, qb), axis=0)
            return acc
        acc = lax.fori_loop(0, npair, body, jnp.zeros((nacc, qb), F32))
        return jnp.sum(acc, axis=0, keepdims=True)

    kf = float(k_sel)
    nacc16 = 4 * 2 * SUBLANES
    min16 = -32768

    def count16(pred_fn):
        def body(cp, acc):
            for u in range(2):
                ind = jnp.where(pred_fn(half_ref[2 * cp + u]), jnp.int16(1), jnp.int16(0))
                for j in range(kc // nacc16):
                    acc = acc + ind[j * nacc16:(j + 1) * nacc16]
            return acc
        acc = lax.fori_loop(0, npair, body, jnp.zeros((nacc16, qb), I16))
        return jnp.sum(acc.astype(I32), axis=0, keepdims=True).astype(F32)

    def bisect16(above):
        def body(it, lo16):
            cand = lo16 + lax.shift_left(jnp.int32(1), jnp.int32(15) - it)
            cnt = above + count16(lambda hv: hv >= cand.astype(I16))
            return jnp.where(cnt >= kf, cand, lo16)
        return lax.fori_loop(0, 16, body, jnp.full((1, qb), min16, I32))

    hi_t = bisect16(0.0)
    above = count16(lambda hv: hv > hi_t.astype(I16))

    def low_chunk(c):
        key = key_ref[c]
        low = (key & 0xFFFF) + min16
        half_ref[c] = jnp.where((key >> 16) == hi_t, low, min16).astype(I16)

    pair_loop(low_chunk)
    lo_t = bisect16(above)
    lo = lax.shift_left(hi_t, 16) | ((lo_t - min16) & 0xFFFF)
    lo_ref[...] = lo

    cnt_gt = count(lambda k, pos: k > lo)
    cnt_eq = count(lambda k, pos: k == lo)
    need = kf - cnt_gt
    neg_inf_key = INT_MIN + 0x7FFFFF
    bad = jnp.where((cnt_eq > need) & (lo > neg_inf_key), 1.0, 0.0)
    jl_ref[...] = jnp.full((1, qb), seq, I32)

    @pl.when(jnp.max(bad) > 0.0)
    def _():
        nbits = max(1, (seq - 1).bit_length())

        def jbit_body(it, v):
            cand = v + lax.shift_left(jnp.int32(1), jnp.int32(nbits - 1) - it)
            cnt = count(lambda k, pos: (k == lo) & (pos < cand))
            return jnp.where(cnt < need, cand, v)

        jl_ref[...] = lax.fori_loop(0, nbits, jbit_body, jnp.zeros((1, qb), I32))

    hg = 2 * qb
    m_ref[...] = jnp.full(m_ref.shape, NEG_BIG, F32)

    def max_chunk(c):
        kv = ckvn_ref[pl.ds(pl.multiple_of(c * kc, kc), kc), :]
        key = key_ref[c]
        pos = c * kc + sub_pos
        lo_c = lo_ref[...]
        tie = jnp.where(key == lo_c, jnp.where(pos <= jl_ref[...], 0.0, NEG_BIG), NEG_BIG)
        bias = jnp.where(pos <= t_row, jnp.where(key > lo_c, 0.0, tie), NEG_BIG)
        key_ref[c] = pltpu.bitcast(bias, I32)
        bias2 = jnp.concatenate([bias, bias], axis=1)
        for g in range(N_ATT_HEADS // 2):
            cols = slice(g * hg, (g + 1) * hg)
            s = _dot(kv, qlt_ref[:, cols]) + bias2
            m_ref[:, cols] = jnp.maximum(m_ref[:, cols], jnp.max(s, axis=0, keepdims=True))

    pair_loop(max_chunk)

    l_ref[...] = jnp.zeros(l_ref.shape, F32)
    acc_ref[...] = jnp.zeros(acc_ref.shape, F32)

    def att_chunk(c):
        kv = ckvn_ref[pl.ds(pl.multiple_of(c * kc, kc), kc), :]
        kvt = ckvt_ref[c]
        bias = pltpu.bitcast(key_ref[c], F32)
        bias2 = jnp.concatenate([bias, bias], axis=1)
        for g in range(N_ATT_HEADS // 2):
            cols = slice(g * hg, (g + 1) * hg)
            p = jnp.exp2(_dot(kv, qlt_ref[:, cols]) + bias2 - m_ref[:, cols])
            l_ref[:, cols] = l_ref[:, cols] + jnp.sum(p, axis=0, keepdims=True)
            acc_ref[:, cols] = acc_ref[:, cols] + _dot(kvt, p.astype(BF16))

    pair_loop(att_chunk)

    o_t = (acc_ref[...] * (1.0 / l_ref[...])).astype(BF16)
    for h in range(N_ATT_HEADS):
        ot_h = _dot(wuvt_ref[h], o_t[:, h * qb:(h + 1) * qb])
        o_ref[:, h * ATT_HEAD_DIM:(h + 1) * ATT_HEAD_DIM] = jnp.transpose(ot_h).astype(o_ref.dtype)


def _dsa_attention(qidx, qlt, kw, kidx, ckvn, ckvt, w_uvt, bsz, seq):
    t = kw.shape[0]
    qb = Q_BLOCK
    kc = min(DSA_KEY_CHUNK, seq)
    nb = seq // qb
    k_sel = min(TOPK_MAX, seq // 4)
    assert seq % (2 * kc) == 0
    kern = functools.partial(_dsa_attn_kernel, kc=kc, k_sel=k_sel, seq=seq)
    return pl.pallas_call(
        kern,
        out_shape=jax.ShapeDtypeStruct((t, ATT_WIDTH), BF16),
        grid=(bsz, nb),
        in_specs=[
            pl.BlockSpec((None, N_IDX_HEADS, qb, LANES), lambda b, i: (b, 0, i, 0)),
            pl.BlockSpec((None, None, KV_RANK, N_ATT_HEADS * qb), lambda b, i: (b, i, 0, 0)),
            pl.BlockSpec((qb, LANES), lambda b, i: (b * nb + i, 0)),
            pl.BlockSpec((seq, LANES), lambda b, i: (b, 0)),
            pl.BlockSpec((seq, KV_RANK), lambda b, i: (b, 0)),
            pl.BlockSpec((None, seq // kc, KV_RANK, kc), lambda b, i: (b, 0, 0, 0)),
            pl.BlockSpec((N_ATT_HEADS, ATT_HEAD_DIM, KV_RANK), lambda b, i: (0, 0, 0)),
        ],
        out_specs=pl.BlockSpec((qb, ATT_WIDTH), lambda b, i: (b * nb + i, 0)),
        scratch_shapes=[
            pltpu.VMEM((seq // kc, kc, qb), I32),
            pltpu.VMEM((seq // kc, kc, qb), I16),
            pltpu.VMEM((1, N_ATT_HEADS * qb), F32),
            pltpu.VMEM((1, N_ATT_HEADS * qb), F32),
            pltpu.VMEM((KV_RANK, N_ATT_HEADS * qb), F32),
            pltpu.VMEM((1, qb), I32),
            pltpu.VMEM((1, qb), I32),
        ],
        compiler_params=_cparams(("arbitrary", "arbitrary")),
        name="dsa_attention",
    )(qidx, qlt, kw, kidx, ckvn, ckvt, w_uvt)


def _log_sigmoid(x):
    return jnp.minimum(x, 0.0) - jnp.log(1.0 + jnp.exp(-jnp.abs(x)))


def _mlstm_kernel(q_ref, k_ref, v_ref, og_ref, kw_ref, gb_ref, gn_ref, out_ref,
                  c_ref, n_ref, m_ref, *, chunk):
    lc = chunk
    dk, dv = MLSTM_QK_DIM, MLSTM_V_DIM

    @pl.when(pl.program_id(1) == 0)
    def _():
        c_ref[...] = jnp.zeros(c_ref.shape, F32)
        n_ref[...] = jnp.zeros(n_ref.shape, F32)
        m_ref[...] = jnp.zeros(m_ref.shape, F32)

    kwb = kw_ref[...] + gb_ref[...]
    kwt = jnp.transpose(kwb)
    row = lax.broadcasted_iota(I32, (lc, lc), 0)
    col = lax.broadcasted_iota(I32, (lc, lc), 1)
    causal = col <= row
    tri = jnp.where(causal, 1.0, 0.0)
    tri_t = jnp.where(row <= col, 1.0, 0.0)
    cum_col = _dot_f32(tri, _log_sigmoid(kwb))
    cum_row = _dot_f32(_log_sigmoid(kwt), tri_t)

    for h in range(N_MLSTM_HEADS):
        i_col = kwb[:, KW_MI + h:KW_MI + h + 1]
        i_row = kwt[KW_MI + h:KW_MI + h + 1, :]
        b_col = cum_col[:, KW_MF + h:KW_MF + h + 1]
        b_row = cum_row[KW_MF + h:KW_MF + h + 1, :]
        m_prev = m_ref[h]

        dmat = jnp.where(causal, b_col - b_row + i_row, -jnp.inf)
        m_inter = b_col + m_prev
        m_j = jnp.maximum(m_inter, jnp.max(dmat, axis=1, keepdims=True))
        w_inter = jnp.exp(m_inter - m_j)

        qh = q_ref[:, h * dk:(h + 1) * dk].astype(F32) * (dk ** -0.5)
        kh = k_ref[:, h * dk:(h + 1) * dk].astype(F32)
        vb = v_ref[:, h * dv:(h + 1) * dv]
        qb = qh.astype(BF16)
        sc = _dot_nt(qb, kh.astype(BF16)) * jnp.exp(dmat - m_j)
        c_old = c_ref[h]
        n_old = n_ref[h]
        num = w_inter * _dot(qb, c_old.astype(BF16)) + _dot(sc.astype(BF16), vb)
        den = (w_inter * jnp.sum(qh * n_old, axis=1, keepdims=True)
               + jnp.sum(sc, axis=1, keepdims=True))
        hval = num / jnp.maximum(jnp.abs(den), jnp.exp(-m_j))

        hn = _rms(hval) * gn_ref[:, h * dv:(h + 1) * dv]
        gate = jax.nn.sigmoid(og_ref[:, h * dv:(h + 1) * dv].astype(F32))
        out_ref[:, h * dv:(h + 1) * dv] = (hn * gate).astype(out_ref.dtype)

        b_last = b_col[lc - 1:lc, :]
        g_col = b_last - b_col + i_col
        m_new = jnp.maximum(b_last + m_prev, jnp.max(g_col, axis=0, keepdims=True))
        w_old = jnp.exp(b_last + m_prev - m_new)
        kwgt = kh * jnp.exp(g_col - m_new)
        c_ref[h] = w_old * c_old + _dot(jnp.transpose(kwgt).astype(BF16), vb)
        n_ref[h] = w_old * n_old + jnp.sum(kwgt, axis=0, keepdims=True)
        m_ref[h] = m_new


def _mlstm(z, kw, gate_bias, norm_g, bsz, seq):
    t = z.shape[0]
    lc = min(256, seq)
    nc = seq // lc
    qk_w = N_MLSTM_HEADS * MLSTM_QK_DIM
    kern = functools.partial(_mlstm_kernel, chunk=lc)
    row = lambda b, c: b * nc + c
    return pl.pallas_call(
        kern,
        out_shape=jax.ShapeDtypeStruct((t, MLSTM_WIDTH), BF16),
        grid=(bsz, nc),
        in_specs=[
            pl.BlockSpec((lc, qk_w), lambda b, c: (row(b, c), Z_MQ // qk_w)),
            pl.BlockSpec((lc, qk_w), lambda b, c: (row(b, c), Z_MK // qk_w)),
            pl.BlockSpec((lc, MLSTM_WIDTH), lambda b, c: (row(b, c), Z_MV // MLSTM_WIDTH)),
            pl.BlockSpec((lc, MLSTM_WIDTH), lambda b, c: (row(b, c), Z_MO // MLSTM_WIDTH)),
            pl.BlockSpec((lc, LANES), lambda b, c: (row(b, c), 0)),
            pl.BlockSpec((1, LANES), lambda b, c: (0, 0)),
            pl.BlockSpec((1, MLSTM_WIDTH), lambda b, c: (0, 0)),
        ],
        out_specs=pl.BlockSpec((lc, MLSTM_WIDTH), lambda b, c: (row(b, c), 0)),
        scratch_shapes=[
            pltpu.VMEM((N_MLSTM_HEADS, MLSTM_QK_DIM, MLSTM_V_DIM), F32),
            pltpu.VMEM((N_MLSTM_HEADS, 1, MLSTM_QK_DIM), F32),
            pltpu.VMEM((N_MLSTM_HEADS, 1, 1), F32),
        ],
        compiler_params=_cparams(("arbitrary", "arbitrary")),
        name="mlstm",
    )(z, z, z, z, kw, gate_bias, norm_g)


def _outproj_router_kernel(att_ref, hm_ref, x_ref, g1_ref, gn_ref, sc_ref, sh_ref, wo_ref, rwa_ref, rb_ref,
                           x1_ref, h2_ref, ri_ref, rf_ref, cnt_ref, carry_ref, *, tm):
    @pl.when(pl.program_id(0) == 0)
    def _():
        carry_ref[...] = jnp.zeros(carry_ref.shape, F32)

    y = _dot(att_ref[...], wo_ref[:ATT_WIDTH, :]) + _dot(hm_ref[...], wo_ref[ATT_WIDTH:, :])
    x1 = x_ref[...] + g1_ref[...] * y
    x1_ref[...] = x1
    h2 = _rms(x1) * gn_ref[...] * (1.0 + sc_ref[...]) + sh_ref[...]
    h2_ref[...] = h2

    h2_hi = h2.astype(BF16)
    h2_lo = (h2 - h2_hi.astype(F32)).astype(BF16)
    both = _dot(h2_hi, rwa_ref[...])
    logits = both[:, :LANES] + both[:, LANES:] + _dot(h2_lo, rwa_ref[:, :LANES])
    aff = jax.nn.sigmoid(jnp.transpose(logits)[:N_EXPERTS, :])
    sel = aff + rb_ref[...]

    epg = EXPERTS_PER_GROUP
    riota = lax.broadcasted_iota(I32, (epg, tm), 0)
    best = None
    for g in range(N_EXPERT_GROUPS):
        v = sel[g * epg:(g + 1) * epg]
        a = aff[g * epg:(g + 1) * epg]
        m1 = jnp.max(v, axis=0, keepdims=True)
        i1 = jnp.min(jnp.where(v == m1, riota, epg), axis=0, keepdims=True)
        v2 = jnp.where(riota == i1, -jnp.inf, v)
        m2 = jnp.max(v2, axis=0, keepdims=True)
        i2 = jnp.min(jnp.where(v2 == m2, riota, epg), axis=0, keepdims=True)
        a1 = jnp.sum(jnp.where(riota == i1, a, 0.0), axis=0, keepdims=True)
        a2 = jnp.sum(jnp.where(riota == i2, a, 0.0), axis=0, keepdims=True)
        cur = (m1 + m2, i1 + g * epg, i2 + g * epg, a1, a2)
        if best is None:
            best = cur
        else:
            better = cur[0] > best[0]
            best = tuple(jnp.where(better, c_, b_) for c_, b_ in zip(cur, best))
    _, e1, e2, a1, a2 = best
    asum = a1 + a2

    eiota = lax.broadcasted_iota(I32, (N_EXPERTS, tm), 0)
    hit1 = eiota == e1
    hit2 = eiota == e2
    onehot = jnp.where(hit1, 1.0, jnp.where(hit2, 1.0, 0.0))
    srow = lax.broadcasted_iota(I32, (tm, tm), 0)
    scol = lax.broadcasted_iota(I32, (tm, tm), 1)
    before = jnp.where(srow < scol, 1.0, 0.0).astype(BF16)
    rank = _dot(onehot.astype(BF16), before) + carry_ref[...]
    r1 = jnp.sum(jnp.where(hit1, rank, 0.0), axis=0, keepdims=True)
    r2 = jnp.sum(jnp.where(hit2, rank, 0.0), axis=0, keepdims=True)
    carry = carry_ref[...] + jnp.sum(onehot, axis=1, keepdims=True)
    carry_ref[...] = carry
    cnt_ref[...] = jnp.broadcast_to(carry, cnt_ref.shape)

    zi = jnp.zeros((1, tm), I32)
    ri_ref[...] = jnp.concatenate(
        [e1, e2, r1.astype(I32), r2.astype(I32), zi, zi, zi, zi], axis=0)
    zf = jnp.zeros((1, tm), F32)
    rf_ref[...] = jnp.concatenate([a1 / asum, a2 / asum, zf, zf, zf, zf, zf, zf], axis=0)


def _outproj_router(att, hm, x2d, mod_l, gain, w_out_b, router_w, router_b, seq):
    t, d = x2d.shape
    tm = min(256, seq)
    per_b = seq // tm
    nt = t // tm
    kern = functools.partial(_outproj_router_kernel, tm=tm)
    rw = jnp.pad(router_w, ((0, 0), (0, LANES - N_EXPERTS)))
    rw_hi = rw.astype(BF16)
    rw_lo = (rw - rw_hi.astype(F32)).astype(BF16)
    rw_split = jnp.concatenate([rw_hi, rw_lo], axis=1)
    mod_spec = lambda which: pl.BlockSpec((None, None, 1, d), lambda i: (i // per_b, which, 0, 0))
    return pl.pallas_call(
        kern,
        out_shape=(
            jax.ShapeDtypeStruct((t, d), F32),
            jax.ShapeDtypeStruct((t, d), F32),
            jax.ShapeDtypeStruct((nt, SUBLANES, tm), I32),
            jax.ShapeDtypeStruct((nt, SUBLANES, tm), F32),
            jax.ShapeDtypeStruct((N_EXPERTS, LANES), F32),
        ),
        grid=(nt,),
        in_specs=[
            pl.BlockSpec((tm, ATT_WIDTH), lambda i: (i, 0)),
            pl.BlockSpec((tm, MLSTM_WIDTH), lambda i: (i, 0)),
            pl.BlockSpec((tm, d), lambda i: (i, 0)),
            mod_spec(2),
            pl.BlockSpec((1, d), lambda i: (0, 0)),
            mod_spec(4),
            mod_spec(3),
            pl.BlockSpec((ATT_WIDTH + MLSTM_WIDTH, d), lambda i: (0, 0)),
            pl.BlockSpec((d, 2 * LANES), lambda i: (0, 0)),
            pl.BlockSpec((N_EXPERTS, 1), lambda i: (0, 0)),
        ],
        out_specs=(
            pl.BlockSpec((tm, d), lambda i: (i, 0)),
            pl.BlockSpec((tm, d), lambda i: (i, 0)),
            pl.BlockSpec((None, SUBLANES, tm), lambda i: (i, 0, 0)),
            pl.BlockSpec((None, SUBLANES, tm), lambda i: (i, 0, 0)),
            pl.BlockSpec((N_EXPERTS, LANES), lambda i: (0, 0)),
        ),
        scratch_shapes=[pltpu.VMEM((N_EXPERTS, 1), F32)],
        compiler_params=_cparams(("arbitrary",)),
        name="outproj_router",
    )(att, hm, x2d, mod_l, gain.reshape(1, d), mod_l, mod_l, w_out_b, rw_split, router_b.reshape(N_EXPERTS, 1))


def _dispatch_kernel(pos_ref, src_ref, dst_ref, sem, *, tc):
    i = pl.program_id(0)

    def row_copy(r, slot):
        p = pos_ref[2 * (i * tc + r) + slot]
        return pltpu.make_async_copy(src_ref.at[pl.ds(r, 1)], dst_ref.at[pl.ds(p, 1)], sem)

    def start_body(r, carry):
        row_copy(r, 0).start()
        row_copy(r, 1).start()
        return carry

    lax.fori_loop(0, tc, start_body, 0, unroll=8)

    def wait_body(r, carry):
        row_copy(r, 0).wait()
        row_copy(r, 1).wait()
        return carry

    lax.fori_loop(0, tc, wait_body, 0, unroll=8)


def _dispatch(pos, h2, seq):
    t, d = h2.shape
    tc = min(256, seq)
    kern = functools.partial(_dispatch_kernel, tc=tc)
    return pl.pallas_call(
        kern,
        out_shape=jax.ShapeDtypeStruct((2 * t, d), h2.dtype),
        grid_spec=pltpu.PrefetchScalarGridSpec(
            num_scalar_prefetch=1,
            grid=(t // tc,),
            in_specs=[pl.BlockSpec((tc, d), lambda i, pos: (i, 0))],
            out_specs=pl.BlockSpec(memory_space=pl.ANY),
            scratch_shapes=[pltpu.SemaphoreType.DMA],
        ),
        compiler_params=pltpu.CompilerParams(dimension_semantics=("arbitrary",), has_side_effects=True,
                                             vmem_limit_bytes=VMEM_LIMIT),
        name="moe_dispatch",
    )(pos, h2)


def _gmm_kernel(vt_ref, ve_ref, vfirst_ref, vnew_ref, vstart_ref, vend_ref, xs_ref, wgu_ref, wd_ref, ys_ref,
                wgu_b, wd_b, *, tm):
    v = pl.program_id(0)
    start = vstart_ref[v]
    end = vend_ref[v]

    @pl.when(vnew_ref[v] == 1)
    def _():
        wgu_b[...] = wgu_ref[...].astype(BF16)
        wd_b[...] = wd_ref[...].astype(BF16)

    @pl.when(end > start)
    def _():
        x = xs_ref[...].astype(BF16)
        gu = _dot(x, wgu_b[...])
        gt = gu[:, :D_FF_EXPERT]
        up = gu[:, D_FF_EXPERT:]
        act = (gt * jax.nn.sigmoid(gt)) * up
        y = _dot(act.astype(BF16), wd_b[...])
        rows = vt_ref[v] * tm + lax.broadcasted_iota(I32, (tm, 1), 0)
        mine = (rows >= start) & (rows < end)

        @pl.when(vfirst_ref[v] == 1)
        def _():
            ys_ref[...] = jnp.where(mine, y, 0.0)

        @pl.when(vfirst_ref[v] == 0)
        def _():
            ys_ref[...] = jnp.where(mine, y, ys_ref[...])


def _gmm(sched, xs, w_gate_up, w_down, layer, tm):
    rows, d = xs.shape
    n_visits = sched[0].shape[0]
    kern = functools.partial(_gmm_kernel, tm=tm)
    return pl.pallas_call(
        kern,
        out_shape=jax.ShapeDtypeStruct((rows, d), F32),
        grid_spec=pltpu.PrefetchScalarGridSpec(
            num_scalar_prefetch=6,
            grid=(n_visits,),
            in_specs=[
                pl.BlockSpec((tm, d), lambda v, vt, ve, *_: (vt[v], 0)),
                pl.BlockSpec((None, None, d, 2 * D_FF_EXPERT), lambda v, vt, ve, *_: (layer, ve[v], 0, 0)),
                pl.BlockSpec((None, None, D_FF_EXPERT, d), lambda v, vt, ve, *_: (layer, ve[v], 0, 0)),
            ],
            out_specs=pl.BlockSpec((tm, d), lambda v, vt, ve, *_: (vt[v], 0)),
            scratch_shapes=[pltpu.VMEM((d, 2 * D_FF_EXPERT), BF16), pltpu.VMEM((D_FF_EXPERT, d), BF16)],
        ),
        compiler_params=_cparams(("arbitrary",)),
        name="moe_experts",
    )(*sched, xs, w_gate_up, w_down)


def _visit_schedule(counts, n_rows, tm):
    ends = jnp.cumsum(counts)
    starts = ends - counts
    n_tiles = n_rows // tm
    n_visits = n_tiles + N_EXPERTS - 1
    first_tile = starts // tm
    last_tile = jnp.maximum(ends - 1, 0) // tm
    nv = jnp.where(counts > 0, last_tile - first_tile + 1, 0)
    vend_cum = jnp.cumsum(nv)
    voff = vend_cum - nv
    total = vend_cum[-1]
    v = jnp.arange(n_visits, dtype=I32)
    live = v < total
    e = jnp.searchsorted(vend_cum, jnp.minimum(v, total - 1), side="right").astype(I32)
    e = jnp.minimum(e, N_EXPERTS - 1)
    tile = jnp.where(live, first_tile[e] + (v - voff[e]), n_tiles - 1).astype(I32)
    prev_tile = jnp.concatenate([jnp.full((1,), -1, I32), tile[:-1]])
    first = jnp.where(live & (tile != prev_tile), 1, 0).astype(I32)
    prev_e = jnp.concatenate([jnp.full((1,), -1, I32), e[:-1]])
    new_e = jnp.where(live & (e != prev_e), 1, 0).astype(I32)
    vstart = jnp.where(live, starts[e], 0).astype(I32)
    vend = jnp.where(live, ends[e], 0).astype(I32)
    return (tile, e, first, new_e, vstart, vend), starts


def _combine_kernel(pos_ref, ys_ref, x1_ref, ga_ref, gb_ref, g2_ref, fg_ref, o_ref, buf_ref, sems, *, tc, final):
    i = pl.program_id(0)
    slot = i % 2

    def row_copy(tile, dst_slot, r, k):
        p = pos_ref[2 * (tile * tc + r) + k]
        return pltpu.make_async_copy(ys_ref.at[pl.ds(p, 1)], buf_ref.at[dst_slot, k, pl.ds(r, 1)], sems.at[dst_slot])

    def start_tile(tile, dst_slot):
        def body(r, carry):
            row_copy(tile, dst_slot, r, 0).start()
            row_copy(tile, dst_slot, r, 1).start()
            return carry
        lax.fori_loop(0, tc, body, 0, unroll=8)

    def wait_tile(tile, dst_slot):
        def body(r, carry):
            row_copy(tile, dst_slot, r, 0).wait()
            row_copy(tile, dst_slot, r, 1).wait()
            return carry
        lax.fori_loop(0, tc, body, 0, unroll=8)

    @pl.when(i == 0)
    def _():
        start_tile(0, 0)

    @pl.when(i + 1 < pl.num_programs(0))
    def _():
        start_tile(i + 1, 1 - slot)

    wait_tile(i, slot)

    moe = buf_ref[slot, 0] * ga_ref[...] + buf_ref[slot, 1] * gb_ref[...]
    x2 = x1_ref[...] + g2_ref[...] * moe
    if final:
        x2 = _rms(x2) * fg_ref[...]
    o_ref[...] = x2


def _combine(pos, ys, x1, gate_a, gate_b, mod_l, final_g, seq, final):
    t, d = x1.shape
    tc = min(256, seq)
    per_b = seq // tc
    kern = functools.partial(_combine_kernel, tc=tc, final=final)
    return pl.pallas_call(
        kern,
        out_shape=jax.ShapeDtypeStruct((t, d), F32),
        grid_spec=pltpu.PrefetchScalarGridSpec(
            num_scalar_prefetch=1,
            grid=(t // tc,),
            in_specs=[
                pl.BlockSpec(memory_space=pl.ANY),
                pl.BlockSpec((tc, d), lambda i, pos: (i, 0)),
                pl.BlockSpec((tc, 1), lambda i, pos: (i, 0)),
                pl.BlockSpec((tc, 1), lambda i, pos: (i, 0)),
                pl.BlockSpec((None, None, 1, d), lambda i, pos: (i // per_b, 5, 0, 0)),
                pl.BlockSpec((1, d), lambda i, pos: (0, 0)),
            ],
            out_specs=pl.BlockSpec((tc, d), lambda i, pos: (i, 0)),
            scratch_shapes=[pltpu.VMEM((2, 2, tc, d), F32), pltpu.SemaphoreType.DMA((2,))],
        ),
        compiler_params=_cparams(("arbitrary",)),
        name="moe_combine",
    )(pos, ys, x1, gate_a, gate_b, mod_l, final_g.reshape(1, d))


def _rearrange_w_in(w):
    o_kidx = Q_RANK + KV_RANK
    o_mq = o_kidx + IDX_DIM + N_IDX_HEADS
    o_mi = o_mq + 2 * N_MLSTM_HEADS * MLSTM_QK_DIM + 2 * MLSTM_WIDTH
    gates = w[:, o_mi:o_mi + 2 * N_MLSTM_HEADS]
    small = jnp.concatenate([w[:, o_kidx:o_mq], gates], axis=1)
    small = jnp.pad(small, ((0, 0), (0, Z_MQ - Z_KW - small.shape[1])))
    return jnp.concatenate([w[:, :o_kidx], small, w[:, o_mq:o_mi]], axis=1).astype(BF16)


def kernel(x, c, ada_w, ada_b, mix_norm_g, w_in, cq_norm_g, ckv_norm_g, w_uq, w_uk, w_uv, w_qidx,
           mlstm_i_b, mlstm_f_b, mlstm_norm_g, w_out, ffn_norm_g, router_w, router_b, w_gate_up,
           w_down, final_norm_g):
    bsz, seq, d = x.shape
    depth = ada_w.shape[0]
    t = bsz * seq
    assert seq % Q_BLOCK == 0 and seq % min(512, seq) == 0

    mod = _modulation(c, ada_w, ada_b).reshape(depth, bsz, 6, 1, d)
    gmm_tm = 256

    x2d = x.reshape(t, d)
    for l in range(depth):
        mod_l = mod[l]
        z, kw = _in_projection(x2d, mix_norm_g[l], mod_l, _rearrange_w_in(w_in[l]), seq)
        w_ukh = jnp.transpose(w_uk[l], (1, 0, 2)).astype(BF16)
        w_uvt = jnp.transpose(w_uv[l], (1, 2, 0)).astype(BF16)
        w_qidx_p = jnp.pad(w_qidx[l].reshape(Q_RANK, N_IDX_HEADS, IDX_DIM),
                           ((0, 0), (0, 0), (0, LANES - IDX_DIM))).reshape(Q_RANK, N_IDX_HEADS * LANES).astype(BF16)
        qlt, qidx, ckvn, ckvt, kidx = _dsa_prep(z, kw, cq_norm_g[l].reshape(1, Q_RANK),
                                                ckv_norm_g[l].reshape(1, KV_RANK),
                                                w_uq[l].astype(BF16), w_ukh, w_qidx_p, bsz, seq)
        att = _dsa_attention(qidx, qlt, kw, kidx, ckvn, ckvt, w_uvt, bsz, seq)
        gate_bias = jnp.zeros((1, LANES), F32)
        gate_bias = gate_bias.at[0, KW_MI:KW_MI + N_MLSTM_HEADS].set(mlstm_i_b[l])
        gate_bias = gate_bias.at[0, KW_MF:KW_MF + N_MLSTM_HEADS].set(mlstm_f_b[l])
        hm = _mlstm(z, kw, gate_bias, mlstm_norm_g[l].reshape(1, MLSTM_WIDTH), bsz, seq)
        x1, h2, ri, rf, cnt = _outproj_router(att, hm, x2d, mod_l, ffn_norm_g[l], w_out[l].astype(BF16),
                                              router_w, router_b, seq)
        counts = cnt[:, 0].astype(I32)
        sched, starts = _visit_schedule(counts, 2 * t, gmm_tm)
        e1 = ri[:, 0, :].reshape(t)
        e2 = ri[:, 1, :].reshape(t)
        pos = jnp.stack([starts[e1] + ri[:, 2, :].reshape(t), starts[e2] + ri[:, 3, :].reshape(t)],
                        axis=-1).reshape(2 * t).astype(I32)
        xs = _dispatch(pos, h2, seq)
        ys = _gmm(sched, xs, w_gate_up, w_down, l, gmm_tm)
        x2d = _combine(pos, ys, x1, rf[:, 0, :].reshape(t, 1), rf[:, 1, :].reshape(t, 1), mod_l,
                       final_norm_g, seq, final=(l == depth - 1))
    return x2d.reshape(bsz, seq, d)
```

```python
import functools

import jax
import jax.numpy as jnp
from jax import lax
from jax.experimental import pallas as pl
from jax.experimental.pallas import tpu as pltpu

F32 = jnp.float32
BF16 = jnp.bfloat16
I32 = jnp.int32
I16 = jnp.int16

EPS = 1e-6

N_ATT_HEADS = 8
ATT_HEAD_DIM = 128
Q_RANK = 512
KV_RANK = 256
N_IDX_HEADS = 8
IDX_DIM = 64
TOPK_MAX = 256
Q_BLOCK = 128
N_MLSTM_HEADS = 4
MLSTM_QK_DIM = 128
MLSTM_V_DIM = 256
N_EXPERT_GROUPS = 4
EXPERTS_PER_GROUP = 8
N_EXPERTS = N_EXPERT_GROUPS * EXPERTS_PER_GROUP
D_FF_EXPERT = 512
ATT_WIDTH = N_ATT_HEADS * ATT_HEAD_DIM
MLSTM_WIDTH = N_MLSTM_HEADS * MLSTM_V_DIM

LANES = 128
SUBLANES = 8
VMEM_LIMIT = 52 * 1024 * 1024

Z_CQ = 0
Z_CKV = 512
Z_KW = 768
Z_MQ = 1024
Z_MK = 1536
Z_MV = 2048
Z_MO = 3072
Z_WIDTH = 4096
KW_WIDX = 64
KW_MI = 72
KW_MF = 76

NEG_BIG = -1e30
LOG2E = 1.4426950408889634
DSA_KEY_CHUNK = 256
INT_MIN = -2147483648


def _cparams(sem):
    return pltpu.CompilerParams(dimension_semantics=sem, vmem_limit_bytes=VMEM_LIMIT)


def _rms(x):
    return x * lax.rsqrt(jnp.mean(x * x, axis=-1, keepdims=True) + EPS)


def _dot(a, b):
    return jnp.dot(a, b, preferred_element_type=F32)


def _dot_nt(a, b):
    return lax.dot_general(a, b, (((1,), (1,)), ((), ())), preferred_element_type=F32)


def _dot_f32(a, b):
    return jnp.dot(a, b, preferred_element_type=F32, precision=lax.Precision.HIGHEST)


def _mod_kernel(c_ref, w_ref, b_ref, o_ref):
    c = c_ref[...]
    ca = (c * jax.nn.sigmoid(c)).astype(BF16)
    o_ref[...] = _dot(ca, w_ref[...].astype(BF16)) + b_ref[...]


def _modulation(c, ada_w, ada_b):
    depth, d, n = ada_w.shape
    bsz = c.shape[0]
    rows = ((bsz + SUBLANES - 1) // SUBLANES) * SUBLANES
    c_pad = jnp.pad(c, ((0, rows - bsz), (0, 0)))
    tn = 1024
    out = pl.pallas_call(
        _mod_kernel,
        out_shape=jax.ShapeDtypeStruct((depth, rows, n), F32),
        grid=(depth, n // tn),
        in_specs=[
            pl.BlockSpec((rows, d), lambda l, j: (0, 0)),
            pl.BlockSpec((None, d, tn), lambda l, j: (l, 0, j)),
            pl.BlockSpec((None, 1, tn), lambda l, j: (l, 0, j)),
        ],
        out_specs=pl.BlockSpec((None, rows, tn), lambda l, j: (l, 0, j)),
        compiler_params=_cparams(("arbitrary", "arbitrary")),
        name="ada_modulation",
    )(c_pad, ada_w, ada_b.reshape(depth, 1, n))
    return out[:, :bsz]


def _inproj_kernel(x_ref, g_ref, sc_ref, sh_ref, w_ref, z_ref, kw_ref, *, tn):
    y = _rms(x_ref[...]) * g_ref[...]
    h = (y * (1.0 + sc_ref[...]) + sh_ref[...]).astype(BF16)
    for j in range(Z_WIDTH // tn):
        zj = _dot(h, w_ref[:, j * tn:(j + 1) * tn])
        z_ref[:, j * tn:(j + 1) * tn] = zj.astype(BF16)
        if j == Z_KW // tn:
            kw_ref[...] = zj[:, Z_KW - j * tn:Z_KW - j * tn + LANES]


def _in_projection(x2d, gain, mod_l, w_in_r, seq):
    t, d = x2d.shape
    tm = min(256, seq)
    per_b = seq // tm
    kern = functools.partial(_inproj_kernel, tn=1024)
    return pl.pallas_call(
        kern,
        out_shape=(jax.ShapeDtypeStruct((t, Z_WIDTH), BF16), jax.ShapeDtypeStruct((t, LANES), F32)),
        grid=(t // tm,),
        in_specs=[
            pl.BlockSpec((tm, d), lambda i: (i, 0)),
            pl.BlockSpec((1, d), lambda i: (0, 0)),
            pl.BlockSpec((None, None, 1, d), lambda i: (i // per_b, 1, 0, 0)),
            pl.BlockSpec((None, None, 1, d), lambda i: (i // per_b, 0, 0, 0)),
            pl.BlockSpec((d, Z_WIDTH), lambda i: (0, 0), pipeline_mode=pl.Buffered(1)),
        ],
        out_specs=(pl.BlockSpec((tm, Z_WIDTH), lambda i: (i, 0)), pl.BlockSpec((tm, LANES), lambda i: (i, 0))),
        compiler_params=_cparams(("arbitrary",)),
        name="norm_in_projection",
    )(x2d, gain.reshape(1, d), mod_l, mod_l, w_in_r)


def _dsa_prep_kernel(cq_ref, ckv_ref, kw_ref, gq_ref, gkv_ref, wuq_ref, wuk_ref, wqi_ref,
                     qlt_ref, qidx_ref, ckvn_ref, ckvt_ref, kidx_ref, *, tm, kc):
    qb = Q_BLOCK
    cqn = (_rms(cq_ref[...].astype(F32)) * gq_ref[...]).astype(BF16)
    ckvn = _rms(ckv_ref[...].astype(F32)) * gkv_ref[...]
    ckvn_ref[...] = ckvn.astype(BF16)
    ckvt = jnp.transpose(ckvn).astype(BF16)
    for j in range(tm // kc):
        ckvt_ref[j] = ckvt[:, j * kc:(j + 1) * kc]
    q = _dot(cqn, wuq_ref[...]).astype(BF16)
    qi = _dot(cqn, wqi_ref[...])
    qscale = ATT_HEAD_DIM ** -0.5 * LOG2E
    for h in range(N_ATT_HEADS):
        qlt = _dot_nt(wuk_ref[h], q[:, h * ATT_HEAD_DIM:(h + 1) * ATT_HEAD_DIM]) * qscale
        for j in range(tm // qb):
            qlt_ref[j, :, h * qb:(h + 1) * qb] = qlt[:, j * qb:(j + 1) * qb].astype(BF16)
    for h in range(N_IDX_HEADS):
        qidx_ref[h] = qi[:, h * LANES:(h + 1) * LANES].astype(BF16)
    kw = kw_ref[...]
    lane = lax.broadcasted_iota(I32, kw.shape, 1)
    kidx_ref[...] = jnp.where(lane < IDX_DIM, kw, 0.0).astype(BF16)


def _dsa_prep(z, kw, gq, gkv, w_uq, w_ukh, w_qidx_p, bsz, seq):
    t = z.shape[0]
    tm = min(256, seq)
    kc = min(DSA_KEY_CHUNK, seq)
    qb = Q_BLOCK
    per_b = seq // tm
    kern = functools.partial(_dsa_prep_kernel, tm=tm, kc=kc)
    return pl.pallas_call(
        kern,
        out_shape=(
            jax.ShapeDtypeStruct((bsz, seq // qb, KV_RANK, N_ATT_HEADS * qb), BF16),
            jax.ShapeDtypeStruct((bsz, N_IDX_HEADS, seq, LANES), BF16),
            jax.ShapeDtypeStruct((t, KV_RANK), BF16),
            jax.ShapeDtypeStruct((bsz, seq // kc, KV_RANK, kc), BF16),
            jax.ShapeDtypeStruct((t, LANES), BF16),
        ),
        grid=(t // tm,),
        in_specs=[
            pl.BlockSpec((tm, Q_RANK), lambda i: (i, Z_CQ // Q_RANK)),
            pl.BlockSpec((tm, KV_RANK), lambda i: (i, Z_CKV // KV_RANK)),
            pl.BlockSpec((tm, LANES), lambda i: (i, 0)),
            pl.BlockSpec((1, Q_RANK), lambda i: (0, 0)),
            pl.BlockSpec((1, KV_RANK), lambda i: (0, 0)),
            pl.BlockSpec((Q_RANK, ATT_WIDTH), lambda i: (0, 0)),
            pl.BlockSpec((N_ATT_HEADS, KV_RANK, ATT_HEAD_DIM), lambda i: (0, 0, 0)),
            pl.BlockSpec((Q_RANK, N_IDX_HEADS * LANES), lambda i: (0, 0)),
        ],
        out_specs=(
            pl.BlockSpec((None, tm // qb, KV_RANK, N_ATT_HEADS * qb), lambda i: (i // per_b, i % per_b, 0, 0)),
            pl.BlockSpec((None, N_IDX_HEADS, tm, LANES), lambda i: (i // per_b, 0, i % per_b, 0)),
            pl.BlockSpec((tm, KV_RANK), lambda i: (i, 0)),
            pl.BlockSpec((None, tm // kc, KV_RANK, kc), lambda i: (i // per_b, i % per_b, 0, 0)),
            pl.BlockSpec((tm, LANES), lambda i: (i, 0)),
        ),
        compiler_params=_cparams(("arbitrary",)),
        name="dsa_prep",
    )(z, z, kw, gq, gkv, w_uq, w_ukh, w_qidx_p)


def _dsa_attn_kernel(qidx_ref, qlt_ref, kw_ref, kidx_ref, ckvn_ref, ckvt_ref, wuvt_ref, o_ref,
                     key_ref, half_ref, m_ref, l_ref, acc_ref, lo_ref, jl_ref, need_ref, bad_ref, *, kc, k_sel, seq):
    qb = Q_BLOCK
    i = pl.program_id(1)
    nkeys = (i + 1) * qb
    npair = (nkeys + 2 * kc - 1) // (2 * kc)
    t_row = i * qb + lax.broadcasted_iota(I32, (1, qb), 1)
    sub_pos = lax.broadcasted_iota(I32, (kc, qb), 0)
    kwt = jnp.transpose(kw_ref[...]) * (N_IDX_HEADS ** -0.5 * IDX_DIM ** -0.5)

    def idx_chunk(c):
        ks = kidx_ref[pl.ds(pl.multiple_of(c * kc, kc), kc), :]
        acc = jnp.zeros((kc, qb), F32)
        for g in range(N_IDX_HEADS // 2):
            r = _dot_nt(ks, qidx_ref[2 * g:2 * g + 2].reshape(2 * qb, LANES))
            for u in range(2):
                h = 2 * g + u
                acc = acc + kwt[KW_WIDX + h:KW_WIDX + h + 1, :] * jnp.maximum(r[:, u * qb:(u + 1) * qb], 0.0)
        pos = c * kc + sub_pos
        acc = jnp.where(pos <= t_row, acc + 0.0, -jnp.inf)
        bits = pltpu.bitcast(acc, I32)
        key = bits ^ ((bits >> 31) & 0x7FFFFFFF)
        key_ref[c] = key
        half_ref[c] = (key >> 16).astype(I16)

    nquad = npair // 2

    def pair_loop(chunk_fn):
        def body(cq, carry):
            for u in range(4):
                chunk_fn(4 * cq + u)
            return carry
        lax.fori_loop(0, nquad, body, 0)

        @pl.when(npair % 2 == 1)
        def _():
            chunk_fn(4 * nquad)
            chunk_fn(4 * nquad + 1)

    pair_loop(idx_chunk)

    kf = float(k_sel)
    nacc = 4 * SUBLANES
    nacc16 = 4 * 2 * SUBLANES
    min16 = -32768
    neg_inf_key = INT_MIN + 0x7FFFFF

    def fold(ind, rows, acc):
        for j in range(kc // rows):
            acc = acc + ind[j * rows:(j + 1) * rows]
        return acc

    def select_threshold(n):
        chunks = range(2 * n)

        def count16(pred_fn):
            acc = jnp.zeros((nacc16, qb), I16)
            for c in chunks:
                acc = fold(jnp.where(pred_fn(half_ref[c]), jnp.int16(1), jnp.int16(0)), nacc16, acc)
            return jnp.sum(acc.astype(I32), axis=0, keepdims=True).astype(F32)

        def count32(pred_fn):
            acc = jnp.zeros((nacc, qb), F32)
            for c in chunks:
                acc = fold(jnp.where(pred_fn(key_ref[c]), 1.0, 0.0), nacc, acc)
            return jnp.sum(acc, axis=0, keepdims=True)

        def bisect16(above):
            def body(it, lo16):
                cand = lo16 + lax.shift_left(jnp.int32(1), jnp.int32(15) - it)
                cnt = above + count16(lambda hv: hv >= cand.astype(I16))
                return jnp.where(cnt >= kf, cand, lo16)
            return lax.fori_loop(0, 16, body, jnp.full((1, qb), min16, I32), unroll=2)

        hi_t = bisect16(0.0)
        above = count16(lambda hv: hv > hi_t.astype(I16))
        for c in chunks:
            key = key_ref[c]
            half_ref[c] = jnp.where((key >> 16) == hi_t, (key & 0xFFFF) + min16, min16).astype(I16)
        lo_t = bisect16(above)
        lo = lax.shift_left(hi_t, 16) | ((lo_t - min16) & 0xFFFF)
        lo_ref[...] = lo
        cnt_gt = count32(lambda k: k > lo)
        cnt_eq = count32(lambda k: k == lo)
        need = kf - cnt_gt
        need_ref[...] = need
        bad_ref[...] = jnp.where((cnt_eq > need) & (lo > neg_inf_key), 1.0, 0.0)

    for n in range(1, seq // (2 * kc) + 1):
        pl.when(npair == n)(functools.partial(select_threshold, n))

    jl_ref[...] = jnp.full((1, qb), seq, I32)

    @pl.when(jnp.max(bad_ref[...]) > 0.0)
    def _():
        nbits = max(1, (seq - 1).bit_length())
        lo = lo_ref[...]
        need = need_ref[...]

        def count_ties_before(cand):
            def body(c, acc):
                ind = jnp.where((key_ref[c] == lo) & (c * kc + sub_pos < cand), 1.0, 0.0)
                return fold(ind, nacc, acc)
            acc = lax.fori_loop(0, 2 * npair, body, jnp.zeros((nacc, qb), F32))
            return jnp.sum(acc, axis=0, keepdims=True)

        def jbit_body(it, v):
            cand = v + lax.shift_left(jnp.int32(1), jnp.int32(nbits - 1) - it)
            return jnp.where(count_ties_before(cand) < need, cand, v)

        jl_ref[...] = lax.fori_loop(0, nbits, jbit_body, jnp.zeros((1, qb), I32))

    hg = 2 * qb
    m_ref[...] = jnp.full(m_ref.shape, NEG_BIG, F32)

    def max_chunk(c, plain):
        kv = ckvn_ref[pl.ds(pl.multiple_of(c * kc, kc), kc), :]
        key = key_ref[c]
        lo_c = lo_ref[...]
        if plain:
            bias = jnp.where(key >= lo_c, 0.0, NEG_BIG)
        else:
            pos = c * kc + sub_pos
            tie = jnp.where(key == lo_c, jnp.where(pos <= jl_ref[...], 0.0, NEG_BIG), NEG_BIG)
            bias = jnp.where(pos <= t_row, jnp.where(key > lo_c, 0.0, tie), NEG_BIG)
        key_ref[c] = pltpu.bitcast(bias, I32)
        bias2 = jnp.concatenate([bias, bias], axis=1)
        for g in range(N_ATT_HEADS // 2):
            cols = slice(g * hg, (g + 1) * hg)
            s = _dot(kv, qlt_ref[:, cols]) + bias2
            m_ref[:, cols] = jnp.maximum(m_ref[:, cols], jnp.max(s, axis=0, keepdims=True))

    plain = jnp.logical_and(i >= -(-k_sel // qb), jnp.max(bad_ref[...]) == 0.0)
    pl.when(plain)(lambda: pair_loop(functools.partial(max_chunk, plain=True)))
    pl.when(jnp.logical_not(plain))(lambda: pair_loop(functools.partial(max_chunk, plain=False)))

    l_ref[...] = jnp.zeros(l_ref.shape, F32)
    acc_ref[...] = jnp.zeros(acc_ref.shape, F32)

    def att_chunk(c):
        kv = ckvn_ref[pl.ds(pl.multiple_of(c * kc, kc), kc), :]
        kvt = ckvt_ref[c]
        bias = pltpu.bitcast(key_ref[c], F32)
        bias2 = jnp.concatenate([bias, bias], axis=1)
        for g in range(N_ATT_HEADS // 2):
            cols = slice(g * hg, (g + 1) * hg)
            p = jnp.exp2(_dot(kv, qlt_ref[:, cols]) + bias2 - m_ref[:, cols])
            l_ref[:, cols] = l_ref[:, cols] + jnp.sum(p, axis=0, keepdims=True)
            acc_ref[:, cols] = acc_ref[:, cols] + _dot(kvt, p.astype(BF16))

    pair_loop(att_chunk)

    o_t = (acc_ref[...] * (1.0 / l_ref[...])).astype(BF16)
    for h in range(N_ATT_HEADS):
        ot_h = _dot(wuvt_ref[h], o_t[:, h * qb:(h + 1) * qb])
        o_ref[:, h * ATT_HEAD_DIM:(h + 1) * ATT_HEAD_DIM] = jnp.transpose(ot_h).astype(o_ref.dtype)


def _dsa_attention(qidx, qlt, kw, kidx, ckvn, ckvt, w_uvt, bsz, seq):
    t = kw.shape[0]
    qb = Q_BLOCK
    kc = min(DSA_KEY_CHUNK, seq)
    nb = seq // qb
    k_sel = min(TOPK_MAX, seq // 4)
    assert seq % (2 * kc) == 0
    kern = functools.partial(_dsa_attn_kernel, kc=kc, k_sel=k_sel, seq=seq)
    return pl.pallas_call(
        kern,
        out_shape=jax.ShapeDtypeStruct((t, ATT_WIDTH), BF16),
        grid=(bsz, nb),
        in_specs=[
            pl.BlockSpec((None, N_IDX_HEADS, qb, LANES), lambda b, i: (b, 0, i, 0)),
            pl.BlockSpec((None, None, KV_RANK, N_ATT_HEADS * qb), lambda b, i: (b, i, 0, 0)),
            pl.BlockSpec((qb, LANES), lambda b, i: (b * nb + i, 0)),
            pl.BlockSpec((seq, LANES), lambda b, i: (b, 0)),
            pl.BlockSpec((seq, KV_RANK), lambda b, i: (b, 0)),
            pl.BlockSpec((None, seq // kc, KV_RANK, kc), lambda b, i: (b, 0, 0, 0)),
            pl.BlockSpec((N_ATT_HEADS, ATT_HEAD_DIM, KV_RANK), lambda b, i: (0, 0, 0)),
        ],
        out_specs=pl.BlockSpec((qb, ATT_WIDTH), lambda b, i: (b * nb + i, 0)),
        scratch_shapes=[
            pltpu.VMEM((seq // kc, kc, qb), I32),
            pltpu.VMEM((seq // kc, kc, qb), I16),
            pltpu.VMEM((1, N_ATT_HEADS * qb), F32),
            pltpu.VMEM((1, N_ATT_HEADS * qb), F32),
            pltpu.VMEM((KV_RANK, N_ATT_HEADS * qb), F32),
            pltpu.VMEM((1, qb), I32),
            pltpu.VMEM((1, qb), I32),
            pltpu.VMEM((1, qb), F32),
            pltpu.VMEM((1, qb), F32),
        ],
        compiler_params=_cparams(("arbitrary", "arbitrary")),
        name="dsa_attention",
    )(qidx, qlt, kw, kidx, ckvn, ckvt, w_uvt)


def _log_sigmoid(x):
    return jnp.minimum(x, 0.0) - jnp.log(1.0 + jnp.exp(-jnp.abs(x)))


def _mlstm_kernel(q_ref, k_ref, v_ref, og_ref, kw_ref, gb_ref, gn_ref, out_ref,
                  c_ref, n_ref, m_ref, *, chunk):
    lc = chunk
    dk, dv = MLSTM_QK_DIM, MLSTM_V_DIM

    @pl.when(pl.program_id(1) == 0)
    def _():
        c_ref[...] = jnp.zeros(c_ref.shape, F32)
        n_ref[...] = jnp.zeros(n_ref.shape, F32)
        m_ref[...] = jnp.zeros(m_ref.shape, F32)

    kwb = kw_ref[...] + gb_ref[...]
    kwt = jnp.transpose(kwb)
    row = lax.broadcasted_iota(I32, (lc, lc), 0)
    col = lax.broadcasted_iota(I32, (lc, lc), 1)
    causal = col <= row
    tri = jnp.where(causal, 1.0, 0.0)
    tri_t = jnp.where(row <= col, 1.0, 0.0)
    cum_col = _dot_f32(tri, _log_sigmoid(kwb))
    cum_row = _dot_f32(_log_sigmoid(kwt), tri_t)

    for h in range(N_MLSTM_HEADS):
        i_col = kwb[:, KW_MI + h:KW_MI + h + 1]
        i_row = kwt[KW_MI + h:KW_MI + h + 1, :]
        b_col = cum_col[:, KW_MF + h:KW_MF + h + 1]
        b_row = cum_row[KW_MF + h:KW_MF + h + 1, :]
        m_prev = m_ref[h]

        dmat = jnp.where(causal, b_col - b_row + i_row, -jnp.inf)
        m_inter = b_col + m_prev
        m_j = jnp.maximum(m_inter, jnp.max(dmat, axis=1, keepdims=True))
        w_inter = jnp.exp(m_inter - m_j)

        qh = q_ref[:, h * dk:(h + 1) * dk].astype(F32) * (dk ** -0.5)
        kh = k_ref[:, h * dk:(h + 1) * dk].astype(F32)
        vb = v_ref[:, h * dv:(h + 1) * dv]
        qb = qh.astype(BF16)
        sc = _dot_nt(qb, kh.astype(BF16)) * jnp.exp(dmat - m_j)
        c_old = c_ref[h]
        n_old = n_ref[h]
        num = w_inter * _dot(qb, c_old.astype(BF16)) + _dot(sc.astype(BF16), vb)
        den = (w_inter * jnp.sum(qh * n_old, axis=1, keepdims=True)
               + jnp.sum(sc, axis=1, keepdims=True))
        hval = num / jnp.maximum(jnp.abs(den), jnp.exp(-m_j))

        hn = _rms(hval) * gn_ref[:, h * dv:(h + 1) * dv]
        gate = jax.nn.sigmoid(og_ref[:, h * dv:(h + 1) * dv].astype(F32))
        out_ref[:, h * dv:(h + 1) * dv] = (hn * gate).astype(out_ref.dtype)

        b_last = b_col[lc - 1:lc, :]
        g_col = b_last - b_col + i_col
        m_new = jnp.maximum(b_last + m_prev, jnp.max(g_col, axis=0, keepdims=True))
        w_old = jnp.exp(b_last + m_prev - m_new)
        kwgt = kh * jnp.exp(g_col - m_new)
        c_ref[h] = w_old * c_old + _dot(jnp.transpose(kwgt).astype(BF16), vb)
        n_ref[h] = w_old * n_old + jnp.sum(kwgt, axis=0, keepdims=True)
        m_ref[h] = m_new


def _mlstm(z, kw, gate_bias, norm_g, bsz, seq):
    t = z.shape[0]
    lc = min(256, seq)
    nc = seq // lc
    qk_w = N_MLSTM_HEADS * MLSTM_QK_DIM
    kern = functools.partial(_mlstm_kernel, chunk=lc)
    row = lambda b, c: b * nc + c
    return pl.pallas_call(
        kern,
        out_shape=jax.ShapeDtypeStruct((t, MLSTM_WIDTH), BF16),
        grid=(bsz, nc),
        in_specs=[
            pl.BlockSpec((lc, qk_w), lambda b, c: (row(b, c), Z_MQ // qk_w)),
            pl.BlockSpec((lc, qk_w), lambda b, c: (row(b, c), Z_MK // qk_w)),
            pl.BlockSpec((lc, MLSTM_WIDTH), lambda b, c: (row(b, c), Z_MV // MLSTM_WIDTH)),
            pl.BlockSpec((lc, MLSTM_WIDTH), lambda b, c: (row(b, c), Z_MO // MLSTM_WIDTH)),
            pl.BlockSpec((lc, LANES), lambda b, c: (row(b, c), 0)),
            pl.BlockSpec((1, LANES), lambda b, c: (0, 0)),
            pl.BlockSpec((1, MLSTM_WIDTH), lambda b, c: (0, 0)),
        ],
        out_specs=pl.BlockSpec((lc, MLSTM_WIDTH), lambda b, c: (row(b, c), 0)),
        scratch_shapes=[
            pltpu.VMEM((N_MLSTM_HEADS, MLSTM_QK_DIM, MLSTM_V_DIM), F32),
            pltpu.VMEM((N_MLSTM_HEADS, 1, MLSTM_QK_DIM), F32),
            pltpu.VMEM((N_MLSTM_HEADS, 1, 1), F32),
        ],
        compiler_params=_cparams(("arbitrary", "arbitrary")),
        name="mlstm",
    )(z, z, z, z, kw, gate_bias, norm_g)


def _outproj_router_kernel(att_ref, hm_ref, x_ref, g1_ref, gn_ref, sc_ref, sh_ref, wo_ref, rwa_ref, rb_ref,
                           x1_ref, h2_ref, ri_ref, rf_ref, cnt_ref, carry_ref, *, tm):
    @pl.when(pl.program_id(0) == 0)
    def _():
        carry_ref[...] = jnp.zeros(carry_ref.shape, F32)

    y = _dot(att_ref[...], wo_ref[:ATT_WIDTH, :]) + _dot(hm_ref[...], wo_ref[ATT_WIDTH:, :])
    x1 = x_ref[...] + g1_ref[...] * y
    x1_ref[...] = x1
    h2 = _rms(x1) * gn_ref[...] * (1.0 + sc_ref[...]) + sh_ref[...]
    h2_ref[...] = h2

    h2_hi = h2.astype(BF16)
    h2_lo = (h2 - h2_hi.astype(F32)).astype(BF16)
    both = _dot(h2_hi, rwa_ref[...])
    logits = both[:, :LANES] + both[:, LANES:] + _dot(h2_lo, rwa_ref[:, :LANES])
    aff = jax.nn.sigmoid(jnp.transpose(logits)[:N_EXPERTS, :])
    sel = aff + rb_ref[...]

    epg = EXPERTS_PER_GROUP
    riota = lax.broadcasted_iota(I32, (epg, tm), 0)
    best = None
    for g in range(N_EXPERT_GROUPS):
        v = sel[g * epg:(g + 1) * epg]
        a = aff[g * epg:(g + 1) * epg]
        m1 = jnp.max(v, axis=0, keepdims=True)
        i1 = jnp.min(jnp.where(v == m1, riota, epg), axis=0, keepdims=True)
        v2 = jnp.where(riota == i1, -jnp.inf, v)
        m2 = jnp.max(v2, axis=0, keepdims=True)
        i2 = jnp.min(jnp.where(v2 == m2, riota, epg), axis=0, keepdims=True)
        a1 = jnp.sum(jnp.where(riota == i1, a, 0.0), axis=0, keepdims=True)
        a2 = jnp.sum(jnp.where(riota == i2, a, 0.0), axis=0, keepdims=True)
        cur = (m1 + m2, i1 + g * epg, i2 + g * epg, a1, a2)
        if best is None:
            best = cur
        else:
            better = cur[0] > best[0]
            best = tuple(jnp.where(better, c_, b_) for c_, b_ in zip(cur, best))
    _, e1, e2, a1, a2 = best
    asum = a1 + a2

    eiota = lax.broadcasted_iota(I32, (N_EXPERTS, tm), 0)
    hit1 = eiota == e1
    hit2 = eiota == e2
    onehot = jnp.where(hit1, 1.0, jnp.where(hit2, 1.0, 0.0))
    srow = lax.broadcasted_iota(I32, (tm, tm), 0)
    scol = lax.broadcasted_iota(I32, (tm, tm), 1)
    before = jnp.where(srow < scol, 1.0, 0.0).astype(BF16)
    rank = _dot(onehot.astype(BF16), before) + carry_ref[...]
    r1 = jnp.sum(jnp.where(hit1, rank, 0.0), axis=0, keepdims=True)
    r2 = jnp.sum(jnp.where(hit2, rank, 0.0), axis=0, keepdims=True)
    carry = carry_ref[...] + jnp.sum(onehot, axis=1, keepdims=True)
    carry_ref[...] = carry
    cnt_ref[...] = jnp.broadcast_to(carry, cnt_ref.shape)

    zi = jnp.zeros((1, tm), I32)
    ri_ref[...] = jnp.concatenate(
        [e1, e2, r1.astype(I32), r2.astype(I32), zi, zi, zi, zi], axis=0)
    zf = jnp.zeros((1, tm), F32)
    rf_ref[...] = jnp.concatenate([a1 / asum, a2 / asum, zf, zf, zf, zf, zf, zf], axis=0)


def _outproj_router(att, hm, x2d, mod_l, gain, w_out_b, router_w, router_b, seq):
    t, d = x2d.shape
    tm = min(256, seq)
    per_b = seq // tm
    nt = t // tm
    kern = functools.partial(_outproj_router_kernel, tm=tm)
    rw = jnp.pad(router_w, ((0, 0), (0, LANES - N_EXPERTS)))
    rw_hi = rw.astype(BF16)
    rw_lo = (rw - rw_hi.astype(F32)).astype(BF16)
    rw_split = jnp.concatenate([rw_hi, rw_lo], axis=1)
    mod_spec = lambda which: pl.BlockSpec((None, None, 1, d), lambda i: (i // per_b, which, 0, 0))
    return pl.pallas_call(
        kern,
        out_shape=(
            jax.ShapeDtypeStruct((t, d), F32),
            jax.ShapeDtypeStruct((t, d), F32),
            jax.ShapeDtypeStruct((nt, SUBLANES, tm), I32),
            jax.ShapeDtypeStruct((nt, SUBLANES, tm), F32),
            jax.ShapeDtypeStruct((N_EXPERTS, LANES), F32),
        ),
        grid=(nt,),
        in_specs=[
            pl.BlockSpec((tm, ATT_WIDTH), lambda i: (i, 0)),
            pl.BlockSpec((tm, MLSTM_WIDTH), lambda i: (i, 0)),
            pl.BlockSpec((tm, d), lambda i: (i, 0)),
            mod_spec(2),
            pl.BlockSpec((1, d), lambda i: (0, 0)),
            mod_spec(4),
            mod_spec(3),
            pl.BlockSpec((ATT_WIDTH + MLSTM_WIDTH, d), lambda i: (0, 0)),
            pl.BlockSpec((d, 2 * LANES), lambda i: (0, 0)),
            pl.BlockSpec((N_EXPERTS, 1), lambda i: (0, 0)),
        ],
        out_specs=(
            pl.BlockSpec((tm, d), lambda i: (i, 0)),
            pl.BlockSpec((tm, d), lambda i: (i, 0)),
            pl.BlockSpec((None, SUBLANES, tm), lambda i: (i, 0, 0)),
            pl.BlockSpec((None, SUBLANES, tm), lambda i: (i, 0, 0)),
            pl.BlockSpec((N_EXPERTS, LANES), lambda i: (0, 0)),
        ),
        scratch_shapes=[pltpu.VMEM((N_EXPERTS, 1), F32)],
        compiler_params=_cparams(("arbitrary",)),
        name="outproj_router",
    )(att, hm, x2d, mod_l, gain.reshape(1, d), mod_l, mod_l, w_out_b, rw_split, router_b.reshape(N_EXPERTS, 1))


def _dispatch_kernel(pos_ref, src_ref, dst_ref, sem, *, tc):
    i = pl.program_id(0)

    def row_copy(r, slot):
        p = pos_ref[2 * (i * tc + r) + slot]
        return pltpu.make_async_copy(src_ref.at[pl.ds(r, 1)], dst_ref.at[pl.ds(p, 1)], sem)

    def start_body(r, carry):
        row_copy(r, 0).start()
        row_copy(r, 1).start()
        return carry

    lax.fori_loop(0, tc, start_body, 0, unroll=8)

    def wait_body(r, carry):
        row_copy(r, 0).wait()
        row_copy(r, 1).wait()
        return carry

    lax.fori_loop(0, tc, wait_body, 0, unroll=8)


def _dispatch(pos, h2, seq):
    t, d = h2.shape
    tc = min(256, seq)
    kern = functools.partial(_dispatch_kernel, tc=tc)
    return pl.pallas_call(
        kern,
        out_shape=jax.ShapeDtypeStruct((2 * t, d), h2.dtype),
        grid_spec=pltpu.PrefetchScalarGridSpec(
            num_scalar_prefetch=1,
            grid=(t // tc,),
            in_specs=[pl.BlockSpec((tc, d), lambda i, pos: (i, 0))],
            out_specs=pl.BlockSpec(memory_space=pl.ANY),
            scratch_shapes=[pltpu.SemaphoreType.DMA],
        ),
        compiler_params=pltpu.CompilerParams(dimension_semantics=("arbitrary",), has_side_effects=True,
                                             vmem_limit_bytes=VMEM_LIMIT),
        name="moe_dispatch",
    )(pos, h2)


def _gmm_kernel(vt_ref, ve_ref, vfirst_ref, vnew_ref, vstart_ref, vend_ref, xs_ref, wgu_ref, wd_ref, ys_ref,
                wgu_b, wd_b, *, tm):
    v = pl.program_id(0)
    start = vstart_ref[v]
    end = vend_ref[v]

    @pl.when(vnew_ref[v] == 1)
    def _():
        wgu_b[...] = wgu_ref[...].astype(BF16)
        wd_b[...] = wd_ref[...].astype(BF16)

    @pl.when(end > start)
    def _():
        x = xs_ref[...].astype(BF16)
        gu = _dot(x, wgu_b[...])
        gt = gu[:, :D_FF_EXPERT]
        up = gu[:, D_FF_EXPERT:]
        act = (gt * jax.nn.sigmoid(gt)) * up
        y = _dot(act.astype(BF16), wd_b[...])
        rows = vt_ref[v] * tm + lax.broadcasted_iota(I32, (tm, 1), 0)
        mine = (rows >= start) & (rows < end)

        @pl.when(vfirst_ref[v] == 1)
        def _():
            ys_ref[...] = jnp.where(mine, y, 0.0)

        @pl.when(vfirst_ref[v] == 0)
        def _():
            ys_ref[...] = jnp.where(mine, y, ys_ref[...])


def _gmm(sched, xs, w_gate_up, w_down, layer, tm):
    rows, d = xs.shape
    n_visits = sched[0].shape[0]
    kern = functools.partial(_gmm_kernel, tm=tm)
    return pl.pallas_call(
        kern,
        out_shape=jax.ShapeDtypeStruct((rows, d), F32),
        grid_spec=pltpu.PrefetchScalarGridSpec(
            num_scalar_prefetch=6,
            grid=(n_visits,),
            in_specs=[
                pl.BlockSpec((tm, d), lambda v, vt, ve, *_: (vt[v], 0)),
                pl.BlockSpec((None, None, d, 2 * D_FF_EXPERT), lambda v, vt, ve, *_: (layer, ve[v], 0, 0)),
                pl.BlockSpec((None, None, D_FF_EXPERT, d), lambda v, vt, ve, *_: (layer, ve[v], 0, 0)),
            ],
            out_specs=pl.BlockSpec((tm, d), lambda v, vt, ve, *_: (vt[v], 0)),
            scratch_shapes=[pltpu.VMEM((d, 2 * D_FF_EXPERT), BF16), pltpu.VMEM((D_FF_EXPERT, d), BF16)],
        ),
        compiler_params=_cparams(("arbitrary",)),
        name="moe_experts",
    )(*sched, xs, w_gate_up, w_down)


def _visit_schedule(counts, n_rows, tm):
    ends = jnp.cumsum(counts)
    starts = ends - counts
    n_tiles = n_rows // tm
    n_visits = n_tiles + N_EXPERTS - 1
    first_tile = starts // tm
    last_tile = jnp.maximum(ends - 1, 0) // tm
    nv = jnp.where(counts > 0, last_tile - first_tile + 1, 0)
    vend_cum = jnp.cumsum(nv)
    voff = vend_cum - nv
    total = vend_cum[-1]
    v = jnp.arange(n_visits, dtype=I32)
    live = v < total
    e = jnp.searchsorted(vend_cum, jnp.minimum(v, total - 1), side="right").astype(I32)
    e = jnp.minimum(e, N_EXPERTS - 1)
    tile = jnp.where(live, first_tile[e] + (v - voff[e]), n_tiles - 1).astype(I32)
    prev_tile = jnp.concatenate([jnp.full((1,), -1, I32), tile[:-1]])
    first = jnp.where(live & (tile != prev_tile), 1, 0).astype(I32)
    prev_e = jnp.concatenate([jnp.full((1,), -1, I32), e[:-1]])
    new_e = jnp.where(live & (e != prev_e), 1, 0).astype(I32)
    vstart = jnp.where(live, starts[e], 0).astype(I32)
    vend = jnp.where(live, ends[e], 0).astype(I32)
    return (tile, e, first, new_e, vstart, vend), starts


def _combine_kernel(pos_ref, ys_ref, x1_ref, ga_ref, gb_ref, g2_ref, fg_ref, o_ref, buf_ref, sems, *, tc, final):
    i = pl.program_id(0)
    slot = i % 2

    def row_copy(tile, dst_slot, r, k):
        p = pos_ref[2 * (tile * tc + r) + k]
        return pltpu.make_async_copy(ys_ref.at[pl.ds(p, 1)], buf_ref.at[dst_slot, k, pl.ds(r, 1)], sems.at[dst_slot])

    def start_tile(tile, dst_slot):
        def body(r, carry):
            row_copy(tile, dst_slot, r, 0).start()
            row_copy(tile, dst_slot, r, 1).start()
            return carry
        lax.fori_loop(0, tc, body, 0, unroll=8)

    def wait_tile(tile, dst_slot):
        def body(r, carry):
            row_copy(tile, dst_slot, r, 0).wait()
            row_copy(tile, dst_slot, r, 1).wait()
            return carry
        lax.fori_loop(0, tc, body, 0, unroll=8)

    @pl.when(i == 0)
    def _():
        start_tile(0, 0)

    @pl.when(i + 1 < pl.num_programs(0))
    def _():
        start_tile(i + 1, 1 - slot)

    wait_tile(i, slot)

    moe = buf_ref[slot, 0] * ga_ref[...] + buf_ref[slot, 1] * gb_ref[...]
    x2 = x1_ref[...] + g2_ref[...] * moe
    if final:
        x2 = _rms(x2) * fg_ref[...]
    o_ref[...] = x2


def _combine(pos, ys, x1, gate_a, gate_b, mod_l, final_g, seq, final):
    t, d = x1.shape
    tc = min(256, seq)
    per_b = seq // tc
    kern = functools.partial(_combine_kernel, tc=tc, final=final)
    return pl.pallas_call(
        kern,
        out_shape=jax.ShapeDtypeStruct((t, d), F32),
        grid_spec=pltpu.PrefetchScalarGridSpec(
            num_scalar_prefetch=1,
            grid=(t // tc,),
            in_specs=[
                pl.BlockSpec(memory_space=pl.ANY),
                pl.BlockSpec((tc, d), lambda i, pos: (i, 0)),
                pl.BlockSpec((tc, 1), lambda i, pos: (i, 0)),
                pl.BlockSpec((tc, 1), lambda i, pos: (i, 0)),
                pl.BlockSpec((None, None, 1, d), lambda i, pos: (i // per_b, 5, 0, 0)),
                pl.BlockSpec((1, d), lambda i, pos: (0, 0)),
            ],
            out_specs=pl.BlockSpec((tc, d), lambda i, pos: (i, 0)),
            scratch_shapes=[pltpu.VMEM((2, 2, tc, d), F32), pltpu.SemaphoreType.DMA((2,))],
        ),
        compiler_params=_cparams(("arbitrary",)),
        name="moe_combine",
    )(pos, ys, x1, gate_a, gate_b, mod_l, final_g.reshape(1, d))


def _rearrange_w_in(w):
    o_kidx = Q_RANK + KV_RANK
    o_mq = o_kidx + IDX_DIM + N_IDX_HEADS
    o_mi = o_mq + 2 * N_MLSTM_HEADS * MLSTM_QK_DIM + 2 * MLSTM_WIDTH
    gates = w[:, o_mi:o_mi + 2 * N_MLSTM_HEADS]
    small = jnp.concatenate([w[:, o_kidx:o_mq], gates], axis=1)
    small = jnp.pad(small, ((0, 0), (0, Z_MQ - Z_KW - small.shape[1])))
    return jnp.concatenate([w[:, :o_kidx], small, w[:, o_mq:o_mi]], axis=1).astype(BF16)


def kernel(x, c, ada_w, ada_b, mix_norm_g, w_in, cq_norm_g, ckv_norm_g, w_uq, w_uk, w_uv, w_qidx,
           mlstm_i_b, mlstm_f_b, mlstm_norm_g, w_out, ffn_norm_g, router_w, router_b, w_gate_up,
           w_down, final_norm_g):
    bsz, seq, d = x.shape
    depth = ada_w.shape[0]
    t = bsz * seq
    assert seq % Q_BLOCK == 0 and seq % min(512, seq) == 0

    mod = _modulation(c, ada_w, ada_b).reshape(depth, bsz, 6, 1, d)
    gmm_tm = 256

    x2d = x.reshape(t, d)
    for l in range(depth):
        mod_l = mod[l]
        z, kw = _in_projection(x2d, mix_norm_g[l], mod_l, _rearrange_w_in(w_in[l]), seq)
        w_ukh = jnp.transpose(w_uk[l], (1, 0, 2)).astype(BF16)
        w_uvt = jnp.transpose(w_uv[l], (1, 2, 0)).astype(BF16)
        w_qidx_p = jnp.pad(w_qidx[l].reshape(Q_RANK, N_IDX_HEADS, IDX_DIM),
                           ((0, 0), (0, 0), (0, LANES - IDX_DIM))).reshape(Q_RANK, N_IDX_HEADS * LANES).astype(BF16)
        qlt, qidx, ckvn, ckvt, kidx = _dsa_prep(z, kw, cq_norm_g[l].reshape(1, Q_RANK),
                                                ckv_norm_g[l].reshape(1, KV_RANK),
                                                w_uq[l].astype(BF16), w_ukh, w_qidx_p, bsz, seq)
        att = _dsa_attention(qidx, qlt, kw, kidx, ckvn, ckvt, w_uvt, bsz, seq)
        gate_bias = jnp.zeros((1, LANES), F32)
        gate_bias = gate_bias.at[0, KW_MI:KW_MI + N_MLSTM_HEADS].set(mlstm_i_b[l])
        gate_bias = gate_bias.at[0, KW_MF:KW_MF + N_MLSTM_HEADS].set(mlstm_f_b[l])
        hm = _mlstm(z, kw, gate_bias, mlstm_norm_g[l].reshape(1, MLSTM_WIDTH), bsz, seq)
        x1, h2, ri, rf, cnt = _outproj_router(att, hm, x2d, mod_l, ffn_norm_g[l], w_out[l].astype(BF16),
                                              router_w, router_b, seq)
        counts = cnt[:, 0].astype(I32)
        sched, starts = _visit_schedule(counts, 2 * t, gmm_tm)
        e1 = ri[:, 0, :].reshape(t)
        e2 = ri[:, 1, :].reshape(t)
        pos = jnp.stack([starts[e1] + ri[:, 2, :].reshape(t), starts[e2] + ri[:, 3, :].reshape(t)],
                        axis=-1).reshape(2 * t).astype(I32)
        xs = _dispatch(pos, h2, seq)
        ys = _gmm(sched, xs, w_gate_up, w_down, l, gmm_tm)
        x2d = _combine(pos, ys, x1, rf[:, 0, :].reshape(t, 1), rf[:, 1, :].reshape(t, 1), mod_l,
                       final_norm_g, seq, final=(l == depth - 1))
    return x2d.reshape(bsz, seq, d)
```

```python
import functools

import jax
import jax.numpy as jnp
from jax import lax
from jax.experimental import pallas as pl
from jax.experimental.pallas import tpu as pltpu

F32 = jnp.float32
BF16 = jnp.bfloat16
I32 = jnp.int32
I16 = jnp.int16

EPS = 1e-6

N_ATT_HEADS = 8
ATT_HEAD_DIM = 128
Q_RANK = 512
KV_RANK = 256
N_IDX_HEADS = 8
IDX_DIM = 64
TOPK_MAX = 256
Q_BLOCK = 128
N_MLSTM_HEADS = 4
MLSTM_QK_DIM = 128
MLSTM_V_DIM = 256
N_EXPERT_GROUPS = 4
EXPERTS_PER_GROUP = 8
N_EXPERTS = N_EXPERT_GROUPS * EXPERTS_PER_GROUP
D_FF_EXPERT = 512
ATT_WIDTH = N_ATT_HEADS * ATT_HEAD_DIM
MLSTM_WIDTH = N_MLSTM_HEADS * MLSTM_V_DIM

LANES = 128
SUBLANES = 8
VMEM_LIMIT = 52 * 1024 * 1024

Z_CQ = 0
Z_CKV = 512
Z_KW = 768
Z_MQ = 1024
Z_MK = 1536
Z_MV = 2048
Z_MO = 3072
Z_WIDTH = 4096
KW_WIDX = 64
KW_MI = 72
KW_MF = 76

NEG_BIG = -1e30
LOG2E = 1.4426950408889634
DSA_KEY_CHUNK = 256
INT_MIN = -2147483648


def _cparams(sem):
    return pltpu.CompilerParams(dimension_semantics=sem, vmem_limit_bytes=VMEM_LIMIT)


def _rms(x):
    return x * lax.rsqrt(jnp.mean(x * x, axis=-1, keepdims=True) + EPS)


def _dot(a, b):
    return jnp.dot(a, b, preferred_element_type=F32)


def _dot_nt(a, b):
    return lax.dot_general(a, b, (((1,), (1,)), ((), ())), preferred_element_type=F32)


def _dot_f32(a, b):
    return jnp.dot(a, b, preferred_element_type=F32, precision=lax.Precision.HIGHEST)


def _mod_kernel(c_ref, w_ref, b_ref, o_ref):
    c = c_ref[...]
    ca = (c * jax.nn.sigmoid(c)).astype(BF16)
    o_ref[...] = _dot(ca, w_ref[...].astype(BF16)) + b_ref[...]


def _modulation(c, ada_w, ada_b):
    depth, d, n = ada_w.shape
    bsz = c.shape[0]
    rows = ((bsz + SUBLANES - 1) // SUBLANES) * SUBLANES
    c_pad = jnp.pad(c, ((0, rows - bsz), (0, 0)))
    tn = 1024
    out = pl.pallas_call(
        _mod_kernel,
        out_shape=jax.ShapeDtypeStruct((depth, rows, n), F32),
        grid=(depth, n // tn),
        in_specs=[
            pl.BlockSpec((rows, d), lambda l, j: (0, 0)),
            pl.BlockSpec((None, d, tn), lambda l, j: (l, 0, j)),
            pl.BlockSpec((None, 1, tn), lambda l, j: (l, 0, j)),
        ],
        out_specs=pl.BlockSpec((None, rows, tn), lambda l, j: (l, 0, j)),
        compiler_params=_cparams(("arbitrary", "arbitrary")),
        name="ada_modulation",
    )(c_pad, ada_w, ada_b.reshape(depth, 1, n))
    return out[:, :bsz]


def _inproj_kernel(x_ref, g_ref, sc_ref, sh_ref, w_ref, z_ref, kw_ref, *, tn):
    y = _rms(x_ref[...]) * g_ref[...]
    h = (y * (1.0 + sc_ref[...]) + sh_ref[...]).astype(BF16)
    for j in range(Z_WIDTH // tn):
        zj = _dot(h, w_ref[:, j * tn:(j + 1) * tn])
        z_ref[:, j * tn:(j + 1) * tn] = zj.astype(BF16)
        if j == Z_KW // tn:
            kw_ref[...] = zj[:, Z_KW - j * tn:Z_KW - j * tn + LANES]


def _in_projection(x2d, gain, mod_l, w_in_r, seq):
    t, d = x2d.shape
    tm = min(256, seq)
    per_b = seq // tm
    kern = functools.partial(_inproj_kernel, tn=1024)
    return pl.pallas_call(
        kern,
        out_shape=(jax.ShapeDtypeStruct((t, Z_WIDTH), BF16), jax.ShapeDtypeStruct((t, LANES), F32)),
        grid=(t // tm,),
        in_specs=[
            pl.BlockSpec((tm, d), lambda i: (i, 0)),
            pl.BlockSpec((1, d), lambda i: (0, 0)),
            pl.BlockSpec((None, None, 1, d), lambda i: (i // per_b, 1, 0, 0)),
            pl.BlockSpec((None, None, 1, d), lambda i: (i // per_b, 0, 0, 0)),
            pl.BlockSpec((d, Z_WIDTH), lambda i: (0, 0), pipeline_mode=pl.Buffered(1)),
        ],
        out_specs=(pl.BlockSpec((tm, Z_WIDTH), lambda i: (i, 0)), pl.BlockSpec((tm, LANES), lambda i: (i, 0))),
        compiler_params=_cparams(("arbitrary",)),
        name="norm_in_projection",
    )(x2d, gain.reshape(1, d), mod_l, mod_l, w_in_r)


def _dsa_prep_kernel(cq_ref, ckv_ref, kw_ref, gq_ref, gkv_ref, wuq_ref, wuk_ref, wqi_ref,
                     qlt_ref, qidx_ref, ckvn_ref, ckvt_ref, kidx_ref, *, tm, kc):
    qb = Q_BLOCK
    cqn = (_rms(cq_ref[...].astype(F32)) * gq_ref[...]).astype(BF16)
    ckvn = _rms(ckv_ref[...].astype(F32)) * gkv_ref[...]
    ckvn_ref[...] = ckvn.astype(BF16)
    ckvt = jnp.transpose(ckvn).astype(BF16)
    for j in range(tm // kc):
        ckvt_ref[j] = ckvt[:, j * kc:(j + 1) * kc]
    q = _dot(cqn, wuq_ref[...]).astype(BF16)
    qi = _dot(cqn, wqi_ref[...])
    qscale = ATT_HEAD_DIM ** -0.5 * LOG2E
    for h in range(N_ATT_HEADS):
        qlt = _dot_nt(wuk_ref[h], q[:, h * ATT_HEAD_DIM:(h + 1) * ATT_HEAD_DIM]) * qscale
        for j in range(tm // qb):
            qlt_ref[j, :, h * qb:(h + 1) * qb] = qlt[:, j * qb:(j + 1) * qb].astype(BF16)
    for h in range(N_IDX_HEADS):
        qidx_ref[h] = qi[:, h * LANES:(h + 1) * LANES].astype(BF16)
    kw = kw_ref[...]
    lane = lax.broadcasted_iota(I32, kw.shape, 1)
    kidx_ref[...] = jnp.where(lane < IDX_DIM, kw, 0.0).astype(BF16)


def _dsa_prep(z, kw, gq, gkv, w_uq, w_ukh, w_qidx_p, bsz, seq):
    t = z.shape[0]
    tm = min(256, seq)
    kc = min(DSA_KEY_CHUNK, seq)
    qb = Q_BLOCK
    per_b = seq // tm
    kern = functools.partial(_dsa_prep_kernel, tm=tm, kc=kc)
    return pl.pallas_call(
        kern,
        out_shape=(
            jax.ShapeDtypeStruct((bsz, seq // qb, KV_RANK, N_ATT_HEADS * qb), BF16),
            jax.ShapeDtypeStruct((bsz, N_IDX_HEADS, seq, LANES), BF16),
            jax.ShapeDtypeStruct((t, KV_RANK), BF16),
            jax.ShapeDtypeStruct((bsz, seq // kc, KV_RANK, kc), BF16),
            jax.ShapeDtypeStruct((t, LANES), BF16),
        ),
        grid=(t // tm,),
        in_specs=[
            pl.BlockSpec((tm, Q_RANK), lambda i: (i, Z_CQ // Q_RANK)),
            pl.BlockSpec((tm, KV_RANK), lambda i: (i, Z_CKV // KV_RANK)),
            pl.BlockSpec((tm, LANES), lambda i: (i, 0)),
            pl.BlockSpec((1, Q_RANK), lambda i: (0, 0)),
            pl.BlockSpec((1, KV_RANK), lambda i: (0, 0)),
            pl.BlockSpec((Q_RANK, ATT_WIDTH), lambda i: (0, 0)),
            pl.BlockSpec((N_ATT_HEADS, KV_RANK, ATT_HEAD_DIM), lambda i: (0, 0, 0)),
            pl.BlockSpec((Q_RANK, N_IDX_HEADS * LANES), lambda i: (0, 0)),
        ],
        out_specs=(
            pl.BlockSpec((None, tm // qb, KV_RANK, N_ATT_HEADS * qb), lambda i: (i // per_b, i % per_b, 0, 0)),
            pl.BlockSpec((None, N_IDX_HEADS, tm, LANES), lambda i: (i // per_b, 0, i % per_b, 0)),
            pl.BlockSpec((tm, KV_RANK), lambda i: (i, 0)),
            pl.BlockSpec((None, tm // kc, KV_RANK, kc), lambda i: (i // per_b, i % per_b, 0, 0)),
            pl.BlockSpec((tm, LANES), lambda i: (i, 0)),
        ),
        compiler_params=_cparams(("arbitrary",)),
        name="dsa_prep",
    )(z, z, kw, gq, gkv, w_uq, w_ukh, w_qidx_p)


def _dsa_attn_kernel(qidx_ref, qlt_ref, kw_ref, kidx_ref, ckvn_ref, ckvt_ref, wuvt_ref, o_ref,
                     key_ref, half_ref, m_ref, l_ref, acc_ref, lo_ref, jl_ref, need_ref, bad_ref, *, kc, k_sel, seq):
    qb = Q_BLOCK
    i = pl.program_id(1)
    nkeys = (i + 1) * qb
    npair = (nkeys + 2 * kc - 1) // (2 * kc)
    t_row = i * qb + lax.broadcasted_iota(I32, (1, qb), 1)
    sub_pos = lax.broadcasted_iota(I32, (kc, qb), 0)
    kwt = jnp.transpose(kw_ref[...]) * (N_IDX_HEADS ** -0.5 * IDX_DIM ** -0.5)

    def idx_chunk(c):
        ks = kidx_ref[pl.ds(pl.multiple_of(c * kc, kc), kc), :]
        acc = jnp.zeros((kc, qb), F32)
        for g in range(N_IDX_HEADS // 2):
            r = _dot_nt(ks, qidx_ref[2 * g:2 * g + 2].reshape(2 * qb, LANES))
            for u in range(2):
                h = 2 * g + u
                acc = acc + kwt[KW_WIDX + h:KW_WIDX + h + 1, :] * jnp.maximum(r[:, u * qb:(u + 1) * qb], 0.0)
        pos = c * kc + sub_pos
        acc = jnp.where(pos <= t_row, acc + 0.0, -jnp.inf)
        bits = pltpu.bitcast(acc, I32)
        key = bits ^ ((bits >> 31) & 0x7FFFFFFF)
        key_ref[c] = key
        half_ref[c] = (key >> 16).astype(I16)

    nquad = npair // 2

    def pair_loop(chunk_fn):
        def body(cq, carry):
            for u in range(4):
                chunk_fn(4 * cq + u)
            return carry
        lax.fori_loop(0, nquad, body, 0)

        @pl.when(npair % 2 == 1)
        def _():
            chunk_fn(4 * nquad)
            chunk_fn(4 * nquad + 1)

    pair_loop(idx_chunk)

    kf = float(k_sel)
    nacc = 4 * SUBLANES
    nacc16 = 4 * 2 * SUBLANES
    min16 = -32768
    neg_inf_key = INT_MIN + 0x7FFFFF

    def fold(ind, rows, acc):
        for j in range(kc // rows):
            acc = acc + ind[j * rows:(j + 1) * rows]
        return acc

    def select_threshold(n):
        chunks = range(2 * n)

        def count16(pred_fn):
            acc = jnp.zeros((nacc16, qb), I16)
            for c in chunks:
                acc = fold(jnp.where(pred_fn(half_ref[c]), jnp.int16(1), jnp.int16(0)), nacc16, acc)
            return jnp.sum(acc.astype(I32), axis=0, keepdims=True).astype(F32)

        def count32(pred_fn):
            acc = jnp.zeros((nacc, qb), F32)
            for c in chunks:
                acc = fold(jnp.where(pred_fn(key_ref[c]), 1.0, 0.0), nacc, acc)
            return jnp.sum(acc, axis=0, keepdims=True)

        def bisect16(above):
            def body(it, lo16):
                cand = lo16 + lax.shift_left(jnp.int32(1), jnp.int32(15) - it)
                cnt = above + count16(lambda hv: hv >= cand.astype(I16))
                return jnp.where(cnt >= kf, cand, lo16)
            return lax.fori_loop(0, 16, body, jnp.full((1, qb), min16, I32), unroll=2)

        hi_t = bisect16(0.0)
        above = count16(lambda hv: hv > hi_t.astype(I16))
        for c in chunks:
            key = key_ref[c]
            half_ref[c] = jnp.where((key >> 16) == hi_t, (key & 0xFFFF) + min16, min16).astype(I16)
        lo_t = bisect16(above)
        lo = lax.shift_left(hi_t, 16) | ((lo_t - min16) & 0xFFFF)
        lo_ref[...] = lo
        cnt_gt = count32(lambda k: k > lo)
        cnt_eq = count32(lambda k: k == lo)
        need = kf - cnt_gt
        need_ref[...] = need
        bad_ref[...] = jnp.where((cnt_eq > need) & (lo > neg_inf_key), 1.0, 0.0)

    for n in range(1, seq // (2 * kc) + 1):
        pl.when(npair == n)(functools.partial(select_threshold, n))

    jl_ref[...] = jnp.full((1, qb), seq, I32)

    @pl.when(jnp.max(bad_ref[...]) > 0.0)
    def _():
        nbits = max(1, (seq - 1).bit_length())
        lo = lo_ref[...]
        need = need_ref[...]

        def count_ties_before(cand):
            def body(c, acc):
                ind = jnp.where((key_ref[c] == lo) & (c * kc + sub_pos < cand), 1.0, 0.0)
                return fold(ind, nacc, acc)
            acc = lax.fori_loop(0, 2 * npair, body, jnp.zeros((nacc, qb), F32))
            return jnp.sum(acc, axis=0, keepdims=True)

        def jbit_body(it, v):
            cand = v + lax.shift_left(jnp.int32(1), jnp.int32(nbits - 1) - it)
            return jnp.where(count_ties_before(cand) < need, cand, v)

        jl_ref[...] = lax.fori_loop(0, nbits, jbit_body, jnp.zeros((1, qb), I32))

    hg = 2 * qb
    m_ref[...] = jnp.full(m_ref.shape, NEG_BIG, F32)

    def max_chunk(c):
        kv = ckvn_ref[pl.ds(pl.multiple_of(c * kc, kc), kc), :]
        key = key_ref[c]
        pos = c * kc + sub_pos
        lo_c = lo_ref[...]
        tie = jnp.where(key == lo_c, jnp.where(pos <= jl_ref[...], 0.0, NEG_BIG), NEG_BIG)
        bias = jnp.where(pos <= t_row, jnp.where(key > lo_c, 0.0, tie), NEG_BIG)
        key_ref[c] = pltpu.bitcast(bias, I32)
        bias2 = jnp.concatenate([bias, bias], axis=1)
        for g in range(N_ATT_HEADS // 2):
            cols = slice(g * hg, (g + 1) * hg)
            s = _dot(kv, qlt_ref[:, cols]) + bias2
            m_ref[:, cols] = jnp.maximum(m_ref[:, cols], jnp.max(s, axis=0, keepdims=True))

    pair_loop(max_chunk)

    l_ref[...] = jnp.zeros(l_ref.shape, F32)
    acc_ref[...] = jnp.zeros(acc_ref.shape, F32)

    def att_chunk(c):
        kv = ckvn_ref[pl.ds(pl.multiple_of(c * kc, kc), kc), :]
        kvt = ckvt_ref[c]
        bias = pltpu.bitcast(key_ref[c], F32)
        bias2 = jnp.concatenate([bias, bias], axis=1)
        for g in range(N_ATT_HEADS // 2):
            cols = slice(g * hg, (g + 1) * hg)
            p = jnp.exp2(_dot(kv, qlt_ref[:, cols]) + bias2 - m_ref[:, cols])
            l_ref[:, cols] = l_ref[:, cols] + jnp.sum(p, axis=0, keepdims=True)
            acc_ref[:, cols] = acc_ref[:, cols] + _dot(kvt, p.astype(BF16))

    pair_loop(att_chunk)

    o_t = (acc_ref[...] * (1.0 / l_ref[...])).astype(BF16)
    for h in range(N_ATT_HEADS):
        ot_h = _dot(wuvt_ref[h], o_t[:, h * qb:(h + 1) * qb])
        o_ref[:, h * ATT_HEAD_DIM:(h + 1) * ATT_HEAD_DIM] = jnp.transpose(ot_h).astype(o_ref.dtype)


def _dsa_attention(qidx, qlt, kw, kidx, ckvn, ckvt, w_uvt, bsz, seq):
    t = kw.shape[0]
    qb = Q_BLOCK
    kc = min(DSA_KEY_CHUNK, seq)
    nb = seq // qb
    k_sel = min(TOPK_MAX, seq // 4)
    assert seq % (2 * kc) == 0
    kern = functools.partial(_dsa_attn_kernel, kc=kc, k_sel=k_sel, seq=seq)
    return pl.pallas_call(
        kern,
        out_shape=jax.ShapeDtypeStruct((t, ATT_WIDTH), BF16),
        grid=(bsz, nb),
        in_specs=[
            pl.BlockSpec((None, N_IDX_HEADS, qb, LANES), lambda b, i: (b, 0, i, 0)),
            pl.BlockSpec((None, None, KV_RANK, N_ATT_HEADS * qb), lambda b, i: (b, i, 0, 0)),
            pl.BlockSpec((qb, LANES), lambda b, i: (b * nb + i, 0)),
            pl.BlockSpec((seq, LANES), lambda b, i: (b, 0)),
            pl.BlockSpec((seq, KV_RANK), lambda b, i: (b, 0)),
            pl.BlockSpec((None, seq // kc, KV_RANK, kc), lambda b, i: (b, 0, 0, 0)),
            pl.BlockSpec((N_ATT_HEADS, ATT_HEAD_DIM, KV_RANK), lambda b, i: (0, 0, 0)),
        ],
        out_specs=pl.BlockSpec((qb, ATT_WIDTH), lambda b, i: (b * nb + i, 0)),
        scratch_shapes=[
            pltpu.VMEM((seq // kc, kc, qb), I32),
            pltpu.VMEM((seq // kc, kc, qb), I16),
            pltpu.VMEM((1, N_ATT_HEADS * qb), F32),
            pltpu.VMEM((1, N_ATT_HEADS * qb), F32),
            pltpu.VMEM((KV_RANK, N_ATT_HEADS * qb), F32),
            pltpu.VMEM((1, qb), I32),
            pltpu.VMEM((1, qb), I32),
            pltpu.VMEM((1, qb), F32),
            pltpu.VMEM((1, qb), F32),
        ],
        compiler_params=_cparams(("arbitrary", "arbitrary")),
        name="dsa_attention",
    )(qidx, qlt, kw, kidx, ckvn, ckvt, w_uvt)


def _log_sigmoid(x):
    return jnp.minimum(x, 0.0) - jnp.log(1.0 + jnp.exp(-jnp.abs(x)))


def _mlstm_kernel(q_ref, k_ref, v_ref, og_ref, kw_ref, gb_ref, gn_ref, out_ref,
                  c_ref, n_ref, m_ref, *, chunk):
    lc = chunk
    dk, dv = MLSTM_QK_DIM, MLSTM_V_DIM

    @pl.when(pl.program_id(1) == 0)
    def _():
        c_ref[...] = jnp.zeros(c_ref.shape, F32)
        n_ref[...] = jnp.zeros(n_ref.shape, F32)
        m_ref[...] = jnp.zeros(m_ref.shape, F32)

    kwb = kw_ref[...] + gb_ref[...]
    kwt = jnp.transpose(kwb)
    row = lax.broadcasted_iota(I32, (lc, lc), 0)
    col = lax.broadcasted_iota(I32, (lc, lc), 1)
    causal = col <= row
    tri = jnp.where(causal, 1.0, 0.0)
    tri_t = jnp.where(row <= col, 1.0, 0.0)
    cum_col = _dot_f32(tri, _log_sigmoid(kwb))
    cum_row = _dot_f32(_log_sigmoid(kwt), tri_t)

    for h in range(N_MLSTM_HEADS):
        i_col = kwb[:, KW_MI + h:KW_MI + h + 1]
        i_row = kwt[KW_MI + h:KW_MI + h + 1, :]
        b_col = cum_col[:, KW_MF + h:KW_MF + h + 1]
        b_row = cum_row[KW_MF + h:KW_MF + h + 1, :]
        m_prev = m_ref[h]

        dmat = jnp.where(causal, b_col - b_row + i_row, -jnp.inf)
        m_inter = b_col + m_prev
        m_j = jnp.maximum(m_inter, jnp.max(dmat, axis=1, keepdims=True))
        w_inter = jnp.exp(m_inter - m_j)

        qh = q_ref[:, h * dk:(h + 1) * dk].astype(F32) * (dk ** -0.5)
        kh = k_ref[:, h * dk:(h + 1) * dk].astype(F32)
        vb = v_ref[:, h * dv:(h + 1) * dv]
        qb = qh.astype(BF16)
        sc = _dot_nt(qb, kh.astype(BF16)) * jnp.exp(dmat - m_j)
        c_old = c_ref[h]
        n_old = n_ref[h]
        num = w_inter * _dot(qb, c_old.astype(BF16)) + _dot(sc.astype(BF16), vb)
        den = (w_inter * jnp.sum(qh * n_old, axis=1, keepdims=True)
               + jnp.sum(sc, axis=1, keepdims=True))
        hval = num / jnp.maximum(jnp.abs(den), jnp.exp(-m_j))

        hn = _rms(hval) * gn_ref[:, h * dv:(h + 1) * dv]
        gate = jax.nn.sigmoid(og_ref[:, h * dv:(h + 1) * dv].astype(F32))
        out_ref[:, h * dv:(h + 1) * dv] = (hn * gate).astype(out_ref.dtype)

        b_last = b_col[lc - 1:lc, :]
        g_col = b_last - b_col + i_col
        m_new = jnp.maximum(b_last + m_prev, jnp.max(g_col, axis=0, keepdims=True))
        w_old = jnp.exp(b_last + m_prev - m_new)
        kwgt = kh * jnp.exp(g_col - m_new)
        c_ref[h] = w_old * c_old + _dot(jnp.transpose(kwgt).astype(BF16), vb)
        n_ref[h] = w_old * n_old + jnp.sum(kwgt, axis=0, keepdims=True)
        m_ref[h] = m_new


def _mlstm(z, kw, gate_bias, norm_g, bsz, seq):
    t = z.shape[0]
    lc = min(256, seq)
    nc = seq // lc
    qk_w = N_MLSTM_HEADS * MLSTM_QK_DIM
    kern = functools.partial(_mlstm_kernel, chunk=lc)
    row = lambda b, c: b * nc + c
    return pl.pallas_call(
        kern,
        out_shape=jax.ShapeDtypeStruct((t, MLSTM_WIDTH), BF16),
        grid=(bsz, nc),
        in_specs=[
            pl.BlockSpec((lc, qk_w), lambda b, c: (row(b, c), Z_MQ // qk_w)),
            pl.BlockSpec((lc, qk_w), lambda b, c: (row(b, c), Z_MK // qk_w)),
            pl.BlockSpec((lc, MLSTM_WIDTH), lambda b, c: (row(b, c), Z_MV // MLSTM_WIDTH)),
            pl.BlockSpec((lc, MLSTM_WIDTH), lambda b, c: (row(b, c), Z_MO // MLSTM_WIDTH)),
            pl.BlockSpec((lc, LANES), lambda b, c: (row(b, c), 0)),
            pl.BlockSpec((1, LANES), lambda b, c: (0, 0)),
            pl.BlockSpec((1, MLSTM_WIDTH), lambda b, c: (0, 0)),
        ],
        out_specs=pl.BlockSpec((lc, MLSTM_WIDTH), lambda b, c: (row(b, c), 0)),
        scratch_shapes=[
            pltpu.VMEM((N_MLSTM_HEADS, MLSTM_QK_DIM, MLSTM_V_DIM), F32),
            pltpu.VMEM((N_MLSTM_HEADS, 1, MLSTM_QK_DIM), F32),
            pltpu.VMEM((N_MLSTM_HEADS, 1, 1), F32),
        ],
        compiler_params=_cparams(("arbitrary", "arbitrary")),
        name="mlstm",
    )(z, z, z, z, kw, gate_bias, norm_g)


def _outproj_router_kernel(att_ref, hm_ref, x_ref, g1_ref, gn_ref, sc_ref, sh_ref, wo_ref, rwa_ref, rb_ref,
                           x1_ref, h2_ref, ri_ref, rf_ref, cnt_ref, carry_ref, *, tm):
    @pl.when(pl.program_id(0) == 0)
    def _():
        carry_ref[...] = jnp.zeros(carry_ref.shape, F32)

    y = _dot(att_ref[...], wo_ref[:ATT_WIDTH, :]) + _dot(hm_ref[...], wo_ref[ATT_WIDTH:, :])
    x1 = x_ref[...] + g1_ref[...] * y
    x1_ref[...] = x1
    h2 = _rms(x1) * gn_ref[...] * (1.0 + sc_ref[...]) + sh_ref[...]
    h2_ref[...] = h2

    h2_hi = h2.astype(BF16)
    h2_lo = (h2 - h2_hi.astype(F32)).astype(BF16)
    both = _dot(h2_hi, rwa_ref[...])
    logits = both[:, :LANES] + both[:, LANES:] + _dot(h2_lo, rwa_ref[:, :LANES])
    aff = jax.nn.sigmoid(jnp.transpose(logits)[:N_EXPERTS, :])
    sel = aff + rb_ref[...]

    epg = EXPERTS_PER_GROUP
    riota = lax.broadcasted_iota(I32, (epg, tm), 0)
    best = None
    for g in range(N_EXPERT_GROUPS):
        v = sel[g * epg:(g + 1) * epg]
        a = aff[g * epg:(g + 1) * epg]
        m1 = jnp.max(v, axis=0, keepdims=True)
        i1 = jnp.min(jnp.where(v == m1, riota, epg), axis=0, keepdims=True)
        v2 = jnp.where(riota == i1, -jnp.inf, v)
        m2 = jnp.max(v2, axis=0, keepdims=True)
        i2 = jnp.min(jnp.where(v2 == m2, riota, epg), axis=0, keepdims=True)
        a1 = jnp.sum(jnp.where(riota == i1, a, 0.0), axis=0, keepdims=True)
        a2 = jnp.sum(jnp.where(riota == i2, a, 0.0), axis=0, keepdims=True)
        cur = (m1 + m2, i1 + g * epg, i2 + g * epg, a1, a2)
        if best is None:
            best = cur
        else:
            better = cur[0] > best[0]
            best = tuple(jnp.where(better, c_, b_) for c_, b_ in zip(cur, best))
    _, e1, e2, a1, a2 = best
    asum = a1 + a2

    eiota = lax.broadcasted_iota(I32, (N_EXPERTS, tm), 0)
    hit1 = eiota == e1
    hit2 = eiota == e2
    onehot = jnp.where(hit1, 1.0, jnp.where(hit2, 1.0, 0.0))
    srow = lax.broadcasted_iota(I32, (tm, tm), 0)
    scol = lax.broadcasted_iota(I32, (tm, tm), 1)
    before = jnp.where(srow < scol, 1.0, 0.0).astype(BF16)
    rank = _dot(onehot.astype(BF16), before) + carry_ref[...]
    r1 = jnp.sum(jnp.where(hit1, rank, 0.0), axis=0, keepdims=True)
    r2 = jnp.sum(jnp.where(hit2, rank, 0.0), axis=0, keepdims=True)
    carry = carry_ref[...] + jnp.sum(onehot, axis=1, keepdims=True)
    carry_ref[...] = carry
    cnt_ref[...] = jnp.broadcast_to(carry, cnt_ref.shape)

    zi = jnp.zeros((1, tm), I32)
    ri_ref[...] = jnp.concatenate(
        [e1, e2, r1.astype(I32), r2.astype(I32), zi, zi, zi, zi], axis=0)
    zf = jnp.zeros((1, tm), F32)
    rf_ref[...] = jnp.concatenate([a1 / asum, a2 / asum, zf, zf, zf, zf, zf, zf], axis=0)


def _outproj_router(att, hm, x2d, mod_l, gain, w_out_b, router_w, router_b, seq):
    t, d = x2d.shape
    tm = min(256, seq)
    per_b = seq // tm
    nt = t // tm
    kern = functools.partial(_outproj_router_kernel, tm=tm)
    rw = jnp.pad(router_w, ((0, 0), (0, LANES - N_EXPERTS)))
    rw_hi = rw.astype(BF16)
    rw_lo = (rw - rw_hi.astype(F32)).astype(BF16)
    rw_split = jnp.concatenate([rw_hi, rw_lo], axis=1)
    mod_spec = lambda which: pl.BlockSpec((None, None, 1, d), lambda i: (i // per_b, which, 0, 0))
    return pl.pallas_call(
        kern,
        out_shape=(
            jax.ShapeDtypeStruct((t, d), F32),
            jax.ShapeDtypeStruct((t, d), F32),
            jax.ShapeDtypeStruct((nt, SUBLANES, tm), I32),
            jax.ShapeDtypeStruct((nt, SUBLANES, tm), F32),
            jax.ShapeDtypeStruct((N_EXPERTS, LANES), F32),
        ),
        grid=(nt,),
        in_specs=[
            pl.BlockSpec((tm, ATT_WIDTH), lambda i: (i, 0)),
            pl.BlockSpec((tm, MLSTM_WIDTH), lambda i: (i, 0)),
            pl.BlockSpec((tm, d), lambda i: (i, 0)),
            mod_spec(2),
            pl.BlockSpec((1, d), lambda i: (0, 0)),
            mod_spec(4),
            mod_spec(3),
            pl.BlockSpec((ATT_WIDTH + MLSTM_WIDTH, d), lambda i: (0, 0)),
            pl.BlockSpec((d, 2 * LANES), lambda i: (0, 0)),
            pl.BlockSpec((N_EXPERTS, 1), lambda i: (0, 0)),
        ],
        out_specs=(
            pl.BlockSpec((tm, d), lambda i: (i, 0)),
            pl.BlockSpec((tm, d), lambda i: (i, 0)),
            pl.BlockSpec((None, SUBLANES, tm), lambda i: (i, 0, 0)),
            pl.BlockSpec((None, SUBLANES, tm), lambda i: (i, 0, 0)),
            pl.BlockSpec((N_EXPERTS, LANES), lambda i: (0, 0)),
        ),
        scratch_shapes=[pltpu.VMEM((N_EXPERTS, 1), F32)],
        compiler_params=_cparams(("arbitrary",)),
        name="outproj_router",
    )(att, hm, x2d, mod_l, gain.reshape(1, d), mod_l, mod_l, w_out_b, rw_split, router_b.reshape(N_EXPERTS, 1))


def _dispatch_kernel(pos_ref, src_ref, dst_ref, sem, *, tc):
    i = pl.program_id(0)

    def row_copy(r, slot):
        p = pos_ref[2 * (i * tc + r) + slot]
        return pltpu.make_async_copy(src_ref.at[pl.ds(r, 1)], dst_ref.at[pl.ds(p, 1)], sem)

    def start_body(r, carry):
        row_copy(r, 0).start()
        row_copy(r, 1).start()
        return carry

    lax.fori_loop(0, tc, start_body, 0, unroll=8)

    def wait_body(r, carry):
        row_copy(r, 0).wait()
        row_copy(r, 1).wait()
        return carry

    lax.fori_loop(0, tc, wait_body, 0, unroll=8)


def _dispatch(pos, h2, seq):
    t, d = h2.shape
    tc = min(256, seq)
    kern = functools.partial(_dispatch_kernel, tc=tc)
    return pl.pallas_call(
        kern,
        out_shape=jax.ShapeDtypeStruct((2 * t, d), h2.dtype),
        grid_spec=pltpu.PrefetchScalarGridSpec(
            num_scalar_prefetch=1,
            grid=(t // tc,),
            in_specs=[pl.BlockSpec((tc, d), lambda i, pos: (i, 0))],
            out_specs=pl.BlockSpec(memory_space=pl.ANY),
            scratch_shapes=[pltpu.SemaphoreType.DMA],
        ),
        compiler_params=pltpu.CompilerParams(dimension_semantics=("arbitrary",), has_side_effects=True,
                                             vmem_limit_bytes=VMEM_LIMIT),
        name="moe_dispatch",
    )(pos, h2)


def _gmm_kernel(vt_ref, ve_ref, vfirst_ref, vnew_ref, vstart_ref, vend_ref, xs_ref, wgu_ref, wd_ref, ys_ref,
                wgu_b, wd_b, *, tm):
    v = pl.program_id(0)
    start = vstart_ref[v]
    end = vend_ref[v]

    @pl.when(vnew_ref[v] == 1)
    def _():
        wgu_b[...] = wgu_ref[...].astype(BF16)
        wd_b[...] = wd_ref[...].astype(BF16)

    @pl.when(end > start)
    def _():
        x = xs_ref[...].astype(BF16)
        gu = _dot(x, wgu_b[...])
        gt = gu[:, :D_FF_EXPERT]
        up = gu[:, D_FF_EXPERT:]
        act = (gt * jax.nn.sigmoid(gt)) * up
        y = _dot(act.astype(BF16), wd_b[...])
        rows = vt_ref[v] * tm + lax.broadcasted_iota(I32, (tm, 1), 0)
        mine = (rows >= start) & (rows < end)

        @pl.when(vfirst_ref[v] == 1)
        def _():
            ys_ref[...] = jnp.where(mine, y, 0.0)

        @pl.when(vfirst_ref[v] == 0)
        def _():
            ys_ref[...] = jnp.where(mine, y, ys_ref[...])


def _gmm(sched, xs, w_gate_up, w_down, layer, tm):
    rows, d = xs.shape
    n_visits = sched[0].shape[0]
    kern = functools.partial(_gmm_kernel, tm=tm)
    return pl.pallas_call(
        kern,
        out_shape=jax.ShapeDtypeStruct((rows, d), F32),
        grid_spec=pltpu.PrefetchScalarGridSpec(
            num_scalar_prefetch=6,
            grid=(n_visits,),
            in_specs=[
                pl.BlockSpec((tm, d), lambda v, vt, ve, *_: (vt[v], 0)),
                pl.BlockSpec((None, None, d, 2 * D_FF_EXPERT), lambda v, vt, ve, *_: (layer, ve[v], 0, 0)),
                pl.BlockSpec((None, None, D_FF_EXPERT, d), lambda v, vt, ve, *_: (layer, ve[v], 0, 0)),
            ],
            out_specs=pl.BlockSpec((tm, d), lambda v, vt, ve, *_: (vt[v], 0)),
            scratch_shapes=[pltpu.VMEM((d, 2 * D_FF_EXPERT), BF16), pltpu.VMEM((D_FF_EXPERT, d), BF16)],
        ),
        compiler_params=_cparams(("arbitrary",)),
        name="moe_experts",
    )(*sched, xs, w_gate_up, w_down)


def _visit_schedule(counts, n_rows, tm):
    ends = jnp.cumsum(counts)
    starts = ends - counts
    n_tiles = n_rows // tm
    n_visits = n_tiles + N_EXPERTS - 1
    first_tile = starts // tm
    last_tile = jnp.maximum(ends - 1, 0) // tm
    nv = jnp.where(counts > 0, last_tile - first_tile + 1, 0)
    vend_cum = jnp.cumsum(nv)
    voff = vend_cum - nv
    total = vend_cum[-1]
    v = jnp.arange(n_visits, dtype=I32)
    live = v < total
    vq = jnp.minimum(v, total - 1)
    e = jnp.sum((vq[:, None] >= vend_cum[None, :]).astype(I32), axis=1)
    e = jnp.minimum(e, N_EXPERTS - 1)
    owner = e[:, None] == jnp.arange(N_EXPERTS, dtype=I32)[None, :]

    def per_visit(table):
        return jnp.sum(jnp.where(owner, table[None, :], 0), axis=1)

    tile = jnp.where(live, per_visit(first_tile) + (v - per_visit(voff)), n_tiles - 1).astype(I32)
    prev_tile = jnp.concatenate([jnp.full((1,), -1, I32), tile[:-1]])
    first = jnp.where(live & (tile != prev_tile), 1, 0).astype(I32)
    prev_e = jnp.concatenate([jnp.full((1,), -1, I32), e[:-1]])
    new_e = jnp.where(live & (e != prev_e), 1, 0).astype(I32)
    vstart = jnp.where(live, per_visit(starts), 0).astype(I32)
    vend = jnp.where(live, per_visit(ends), 0).astype(I32)
    return (tile, e, first, new_e, vstart, vend), starts


def _combine_kernel(pos_ref, ys_ref, x1_ref, ga_ref, gb_ref, g2_ref, fg_ref, o_ref, buf_ref, sems, *, tc, final):
    i = pl.program_id(0)
    slot = i % 2

    def row_copy(tile, dst_slot, r, k):
        p = pos_ref[2 * (tile * tc + r) + k]
        return pltpu.make_async_copy(ys_ref.at[pl.ds(p, 1)], buf_ref.at[dst_slot, k, pl.ds(r, 1)], sems.at[dst_slot])

    def start_tile(tile, dst_slot):
        def body(r, carry):
            row_copy(tile, dst_slot, r, 0).start()
            row_copy(tile, dst_slot, r, 1).start()
            return carry
        lax.fori_loop(0, tc, body, 0, unroll=8)

    def wait_tile(tile, dst_slot):
        def body(r, carry):
            row_copy(tile, dst_slot, r, 0).wait()
            row_copy(tile, dst_slot, r, 1).wait()
            return carry
        lax.fori_loop(0, tc, body, 0, unroll=8)

    @pl.when(i == 0)
    def _():
        start_tile(0, 0)

    @pl.when(i + 1 < pl.num_programs(0))
    def _():
        start_tile(i + 1, 1 - slot)

    wait_tile(i, slot)

    moe = buf_ref[slot, 0] * ga_ref[...] + buf_ref[slot, 1] * gb_ref[...]
    x2 = x1_ref[...] + g2_ref[...] * moe
    if final:
        x2 = _rms(x2) * fg_ref[...]
    o_ref[...] = x2


def _combine(pos, ys, x1, gate_a, gate_b, mod_l, final_g, seq, final):
    t, d = x1.shape
    tc = min(256, seq)
    per_b = seq // tc
    kern = functools.partial(_combine_kernel, tc=tc, final=final)
    return pl.pallas_call(
        kern,
        out_shape=jax.ShapeDtypeStruct((t, d), F32),
        grid_spec=pltpu.PrefetchScalarGridSpec(
            num_scalar_prefetch=1,
            grid=(t // tc,),
            in_specs=[
                pl.BlockSpec(memory_space=pl.ANY),
                pl.BlockSpec((tc, d), lambda i, pos: (i, 0)),
                pl.BlockSpec((tc, 1), lambda i, pos: (i, 0)),
                pl.BlockSpec((tc, 1), lambda i, pos: (i, 0)),
                pl.BlockSpec((None, None, 1, d), lambda i, pos: (i // per_b, 5, 0, 0)),
                pl.BlockSpec((1, d), lambda i, pos: (0, 0)),
            ],
            out_specs=pl.BlockSpec((tc, d), lambda i, pos: (i, 0)),
            scratch_shapes=[pltpu.VMEM((2, 2, tc, d), F32), pltpu.SemaphoreType.DMA((2,))],
        ),
        compiler_params=_cparams(("arbitrary",)),
        name="moe_combine",
    )(pos, ys, x1, gate_a, gate_b, mod_l, final_g.reshape(1, d))


def _rearrange_w_in(w):
    o_kidx = Q_RANK + KV_RANK
    o_mq = o_kidx + IDX_DIM + N_IDX_HEADS
    o_mi = o_mq + 2 * N_MLSTM_HEADS * MLSTM_QK_DIM + 2 * MLSTM_WIDTH
    gates = w[:, o_mi:o_mi + 2 * N_MLSTM_HEADS]
    small = jnp.concatenate([w[:, o_kidx:o_mq], gates], axis=1)
    small = jnp.pad(small, ((0, 0), (0, Z_MQ - Z_KW - small.shape[1])))
    return jnp.concatenate([w[:, :o_kidx], small, w[:, o_mq:o_mi]], axis=1).astype(BF16)


def kernel(x, c, ada_w, ada_b, mix_norm_g, w_in, cq_norm_g, ckv_norm_g, w_uq, w_uk, w_uv, w_qidx,
           mlstm_i_b, mlstm_f_b, mlstm_norm_g, w_out, ffn_norm_g, router_w, router_b, w_gate_up,
           w_down, final_norm_g):
    bsz, seq, d = x.shape
    depth = ada_w.shape[0]
    t = bsz * seq
    assert seq % Q_BLOCK == 0 and seq % min(512, seq) == 0

    mod = _modulation(c, ada_w, ada_b).reshape(depth, bsz, 6, 1, d)
    gmm_tm = 256

    x2d = x.reshape(t, d)
    for l in range(depth):
        mod_l = mod[l]
        z, kw = _in_projection(x2d, mix_norm_g[l], mod_l, _rearrange_w_in(w_in[l]), seq)
        w_ukh = jnp.transpose(w_uk[l], (1, 0, 2)).astype(BF16)
        w_uvt = jnp.transpose(w_uv[l], (1, 2, 0)).astype(BF16)
        w_qidx_p = jnp.pad(w_qidx[l].reshape(Q_RANK, N_IDX_HEADS, IDX_DIM),
                           ((0, 0), (0, 0), (0, LANES - IDX_DIM))).reshape(Q_RANK, N_IDX_HEADS * LANES).astype(BF16)
        qlt, qidx, ckvn, ckvt, kidx = _dsa_prep(z, kw, cq_norm_g[l].reshape(1, Q_RANK),
                                                ckv_norm_g[l].reshape(1, KV_RANK),
                                                w_uq[l].astype(BF16), w_ukh, w_qidx_p, bsz, seq)
        att = _dsa_attention(qidx, qlt, kw, kidx, ckvn, ckvt, w_uvt, bsz, seq)
        gate_bias = jnp.concatenate([jnp.zeros((KW_MI,), F32), mlstm_i_b[l], mlstm_f_b[l],
                                     jnp.zeros((LANES - KW_MF - N_MLSTM_HEADS,), F32)]).reshape(1, LANES)
        hm = _mlstm(z, kw, gate_bias, mlstm_norm_g[l].reshape(1, MLSTM_WIDTH), bsz, seq)
        x1, h2, ri, rf, cnt = _outproj_router(att, hm, x2d, mod_l, ffn_norm_g[l], w_out[l].astype(BF16),
                                              router_w, router_b, seq)
        counts = cnt[:, 0].astype(I32)
        sched, starts = _visit_schedule(counts, 2 * t, gmm_tm)
        e1 = ri[:, 0, :].reshape(t)
        e2 = ri[:, 1, :].reshape(t)
        pos = jnp.stack([starts[e1] + ri[:, 2, :].reshape(t), starts[e2] + ri[:, 3, :].reshape(t)],
                        axis=-1).reshape(2 * t).astype(I32)
        xs = _dispatch(pos, h2, seq)
        ys = _gmm(sched, xs, w_gate_up, w_down, l, gmm_tm)
        x2d = _combine(pos, ys, x1, rf[:, 0, :].reshape(t, 1), rf[:, 1, :].reshape(t, 1), mod_l,
                       final_norm_g, seq, final=(l == depth - 1))
    return x2d.reshape(bsz, seq, d)
```

```python
import functools

import jax
import jax.numpy as jnp
from jax import lax
from jax.experimental import pallas as pl
from jax.experimental.pallas import tpu as pltpu

F32 = jnp.float32
BF16 = jnp.bfloat16
I32 = jnp.int32
I16 = jnp.int16

EPS = 1e-6

N_ATT_HEADS = 8
ATT_HEAD_DIM = 128
Q_RANK = 512
KV_RANK = 256
N_IDX_HEADS = 8
IDX_DIM = 64
TOPK_MAX = 256
Q_BLOCK = 128
N_MLSTM_HEADS = 4
MLSTM_QK_DIM = 128
MLSTM_V_DIM = 256
N_EXPERT_GROUPS = 4
EXPERTS_PER_GROUP = 8
N_EXPERTS = N_EXPERT_GROUPS * EXPERTS_PER_GROUP
D_FF_EXPERT = 512
ATT_WIDTH = N_ATT_HEADS * ATT_HEAD_DIM
MLSTM_WIDTH = N_MLSTM_HEADS * MLSTM_V_DIM

LANES = 128
SUBLANES = 8
VMEM_LIMIT = 52 * 1024 * 1024

Z_CQ = 0
Z_CKV = 512
Z_KW = 768
Z_MQ = 1024
Z_MK = 1536
Z_MV = 2048
Z_MO = 3072
Z_WIDTH = 4096
KW_WIDX = 64
KW_MI = 72
KW_MF = 76

NEG_BIG = -1e30
LOG2E = 1.4426950408889634
DSA_KEY_CHUNK = 256
INT_MIN = -2147483648


def _cparams(sem):
    return pltpu.CompilerParams(dimension_semantics=sem, vmem_limit_bytes=VMEM_LIMIT)


def _rms(x):
    return x * lax.rsqrt(jnp.mean(x * x, axis=-1, keepdims=True) + EPS)


def _dot(a, b):
    return jnp.dot(a, b, preferred_element_type=F32)


def _dot_nt(a, b):
    return lax.dot_general(a, b, (((1,), (1,)), ((), ())), preferred_element_type=F32)


def _dot_f32(a, b):
    return jnp.dot(a, b, preferred_element_type=F32, precision=lax.Precision.HIGHEST)


def _mod_kernel(c_ref, w_ref, b_ref, o_ref):
    c = c_ref[...]
    ca = (c * jax.nn.sigmoid(c)).astype(BF16)
    o_ref[...] = _dot(ca, w_ref[...].astype(BF16)) + b_ref[...]


def _modulation(c, ada_w, ada_b):
    depth, d, n = ada_w.shape
    bsz = c.shape[0]
    rows = ((bsz + SUBLANES - 1) // SUBLANES) * SUBLANES
    c_pad = jnp.pad(c, ((0, rows - bsz), (0, 0)))
    tn = 1024
    out = pl.pallas_call(
        _mod_kernel,
        out_shape=jax.ShapeDtypeStruct((depth, rows, n), F32),
        grid=(depth, n // tn),
        in_specs=[
            pl.BlockSpec((rows, d), lambda l, j: (0, 0)),
            pl.BlockSpec((None, d, tn), lambda l, j: (l, 0, j)),
            pl.BlockSpec((None, 1, tn), lambda l, j: (l, 0, j)),
        ],
        out_specs=pl.BlockSpec((None, rows, tn), lambda l, j: (l, 0, j)),
        compiler_params=_cparams(("arbitrary", "arbitrary")),
        name="ada_modulation",
    )(c_pad, ada_w, ada_b.reshape(depth, 1, n))
    return out[:, :bsz]


def _inproj_kernel(x_ref, g_ref, sc_ref, sh_ref, w_ref, z_ref, kw_ref, *, tn):
    y = _rms(x_ref[...]) * g_ref[...]
    h = (y * (1.0 + sc_ref[...]) + sh_ref[...]).astype(BF16)
    for j in range(Z_WIDTH // tn):
        zj = _dot(h, w_ref[:, j * tn:(j + 1) * tn])
        z_ref[:, j * tn:(j + 1) * tn] = zj.astype(BF16)
        if j == Z_KW // tn:
            kw_ref[...] = zj[:, Z_KW - j * tn:Z_KW - j * tn + LANES]


def _in_projection(x2d, gain, mod_l, w_in_r, seq):
    t, d = x2d.shape
    tm = min(256, seq)
    per_b = seq // tm
    kern = functools.partial(_inproj_kernel, tn=1024)
    return pl.pallas_call(
        kern,
        out_shape=(jax.ShapeDtypeStruct((t, Z_WIDTH), BF16), jax.ShapeDtypeStruct((t, LANES), F32)),
        grid=(t // tm,),
        in_specs=[
            pl.BlockSpec((tm, d), lambda i: (i, 0)),
            pl.BlockSpec((1, d), lambda i: (0, 0)),
            pl.BlockSpec((None, None, 1, d), lambda i: (i // per_b, 1, 0, 0)),
            pl.BlockSpec((None, None, 1, d), lambda i: (i // per_b, 0, 0, 0)),
            pl.BlockSpec((d, Z_WIDTH), lambda i: (0, 0), pipeline_mode=pl.Buffered(1)),
        ],
        out_specs=(pl.BlockSpec((tm, Z_WIDTH), lambda i: (i, 0)), pl.BlockSpec((tm, LANES), lambda i: (i, 0))),
        compiler_params=_cparams(("arbitrary",)),
        name="norm_in_projection",
    )(x2d, gain.reshape(1, d), mod_l, mod_l, w_in_r)


def _dsa_prep_kernel(cq_ref, ckv_ref, kw_ref, gq_ref, gkv_ref, wuq_ref, wuk_ref, wqi_ref,
                     qlt_ref, qidx_ref, ckvn_ref, ckvt_ref, kidx_ref, *, tm, kc):
    qb = Q_BLOCK
    cqn = (_rms(cq_ref[...].astype(F32)) * gq_ref[...]).astype(BF16)
    ckvn = _rms(ckv_ref[...].astype(F32)) * gkv_ref[...]
    ckvn_ref[...] = ckvn.astype(BF16)
    ckvt = jnp.transpose(ckvn).astype(BF16)
    for j in range(tm // kc):
        ckvt_ref[j] = ckvt[:, j * kc:(j + 1) * kc]
    q = _dot(cqn, wuq_ref[...]).astype(BF16)
    qi = _dot(cqn, wqi_ref[...])
    qscale = ATT_HEAD_DIM ** -0.5 * LOG2E
    for h in range(N_ATT_HEADS):
        qlt = _dot_nt(wuk_ref[h], q[:, h * ATT_HEAD_DIM:(h + 1) * ATT_HEAD_DIM]) * qscale
        for j in range(tm // qb):
            qlt_ref[j, :, h * qb:(h + 1) * qb] = qlt[:, j * qb:(j + 1) * qb].astype(BF16)
    for h in range(N_IDX_HEADS):
        qidx_ref[h] = qi[:, h * LANES:(h + 1) * LANES].astype(BF16)
    kw = kw_ref[...]
    lane = lax.broadcasted_iota(I32, kw.shape, 1)
    kidx_ref[...] = jnp.where(lane < IDX_DIM, kw, 0.0).astype(BF16)


def _dsa_prep(z, kw, gq, gkv, w_uq, w_ukh, w_qidx_p, bsz, seq):
    t = z.shape[0]
    tm = min(256, seq)
    kc = min(DSA_KEY_CHUNK, seq)
    qb = Q_BLOCK
    per_b = seq // tm
    kern = functools.partial(_dsa_prep_kernel, tm=tm, kc=kc)
    return pl.pallas_call(
        kern,
        out_shape=(
            jax.ShapeDtypeStruct((bsz, seq // qb, KV_RANK, N_ATT_HEADS * qb), BF16),
            jax.ShapeDtypeStruct((bsz, N_IDX_HEADS, seq, LANES), BF16),
            jax.ShapeDtypeStruct((t, KV_RANK), BF16),
            jax.ShapeDtypeStruct((bsz, seq // kc, KV_RANK, kc), BF16),
            jax.ShapeDtypeStruct((t, LANES), BF16),
        ),
        grid=(t // tm,),
        in_specs=[
            pl.BlockSpec((tm, Q_RANK), lambda i: (i, Z_CQ // Q_RANK)),
            pl.BlockSpec((tm, KV_RANK), lambda i: (i, Z_CKV // KV_RANK)),
            pl.BlockSpec((tm, LANES), lambda i: (i, 0)),
            pl.BlockSpec((1, Q_RANK), lambda i: (0, 0)),
            pl.BlockSpec((1, KV_RANK), lambda i: (0, 0)),
            pl.BlockSpec((Q_RANK, ATT_WIDTH), lambda i: (0, 0)),
            pl.BlockSpec((N_ATT_HEADS, KV_RANK, ATT_HEAD_DIM), lambda i: (0, 0, 0)),
            pl.BlockSpec((Q_RANK, N_IDX_HEADS * LANES), lambda i: (0, 0)),
        ],
        out_specs=(
            pl.BlockSpec((None, tm // qb, KV_RANK, N_ATT_HEADS * qb), lambda i: (i // per_b, i % per_b, 0, 0)),
            pl.BlockSpec((None, N_IDX_HEADS, tm, LANES), lambda i: (i // per_b, 0, i % per_b, 0)),
            pl.BlockSpec((tm, KV_RANK), lambda i: (i, 0)),
            pl.BlockSpec((None, tm // kc, KV_RANK, kc), lambda i: (i // per_b, i % per_b, 0, 0)),
            pl.BlockSpec((tm, LANES), lambda i: (i, 0)),
        ),
        compiler_params=_cparams(("arbitrary",)),
        name="dsa_prep",
    )(z, z, kw, gq, gkv, w_uq, w_ukh, w_qidx_p)


def _dsa_attn_kernel(qidx_ref, qlt_ref, kw_ref, kidx_ref, ckvn_ref, ckvt_ref, wuvt_ref, o_ref,
                     key_ref, half_ref, m_ref, l_ref, acc_ref, lo_ref, jl_ref, need_ref, bad_ref, *, kc, k_sel, seq):
    qb = Q_BLOCK
    i = pl.program_id(1)
    nkeys = (i + 1) * qb
    npair = (nkeys + 2 * kc - 1) // (2 * kc)
    t_row = i * qb + lax.broadcasted_iota(I32, (1, qb), 1)
    sub_pos = lax.broadcasted_iota(I32, (kc, qb), 0)
    kwt = jnp.transpose(kw_ref[...]) * (N_IDX_HEADS ** -0.5 * IDX_DIM ** -0.5)

    def idx_chunk(c):
        ks = kidx_ref[pl.ds(pl.multiple_of(c * kc, kc), kc), :]
        acc = jnp.zeros((kc, qb), F32)
        for g in range(N_IDX_HEADS // 2):
            r = _dot_nt(ks, qidx_ref[2 * g:2 * g + 2].reshape(2 * qb, LANES))
            for u in range(2):
                h = 2 * g + u
                acc = acc + kwt[KW_WIDX + h:KW_WIDX + h + 1, :] * jnp.maximum(r[:, u * qb:(u + 1) * qb], 0.0)
        pos = c * kc + sub_pos
        acc = jnp.where(pos <= t_row, acc + 0.0, -jnp.inf)
        bits = pltpu.bitcast(acc, I32)
        key = bits ^ ((bits >> 31) & 0x7FFFFFFF)
        key_ref[c] = key
        half_ref[c] = (key >> 16).astype(I16)

    def pair_loop(chunk_fn):
        def body(co, carry):
            for u in range(8):
                chunk_fn(8 * co + u)
            return carry
        lax.fori_loop(0, npair // 4, body, 0)
        base = 8 * (npair // 4)
        rest = npair % 4

        @pl.when(rest >= 2)
        def _():
            for u in range(4):
                chunk_fn(base + u)

        @pl.when(rest % 2 == 1)
        def _():
            tail = base + jnp.where(rest >= 2, 4, 0)
            chunk_fn(tail)
            chunk_fn(tail + 1)

    pair_loop(idx_chunk)

    kf = float(k_sel)
    nacc = 4 * SUBLANES
    nacc16 = 4 * 2 * SUBLANES
    min16 = -32768
    neg_inf_key = INT_MIN + 0x7FFFFF

    def fold(ind, rows, acc):
        for j in range(kc // rows):
            acc = acc + ind[j * rows:(j + 1) * rows]
        return acc

    def select_threshold(n):
        chunks = range(2 * n)

        def count16(pred_fn):
            acc = jnp.zeros((nacc16, qb), I16)
            for c in chunks:
                acc = fold(jnp.where(pred_fn(half_ref[c]), jnp.int16(1), jnp.int16(0)), nacc16, acc)
            return jnp.sum(acc.astype(I32), axis=0, keepdims=True).astype(F32)

        def count32(pred_fn):
            acc = jnp.zeros((nacc, qb), F32)
            for c in chunks:
                acc = fold(jnp.where(pred_fn(key_ref[c]), 1.0, 0.0), nacc, acc)
            return jnp.sum(acc, axis=0, keepdims=True)

        def bisect16(above):
            def body(it, lo16):
                cand = lo16 + lax.shift_left(jnp.int32(1), jnp.int32(15) - it)
                cnt = above + count16(lambda hv: hv >= cand.astype(I16))
                return jnp.where(cnt >= kf, cand, lo16)
            return lax.fori_loop(0, 16, body, jnp.full((1, qb), min16, I32), unroll=2)

        hi_t = bisect16(0.0)
        above = count16(lambda hv: hv > hi_t.astype(I16))
        for c in chunks:
            key = key_ref[c]
            half_ref[c] = jnp.where((key >> 16) == hi_t, (key & 0xFFFF) + min16, min16).astype(I16)
        lo_t = bisect16(above)
        lo = lax.shift_left(hi_t, 16) | ((lo_t - min16) & 0xFFFF)
        lo_ref[...] = lo
        cnt_gt = count32(lambda k: k > lo)
        cnt_eq = count32(lambda k: k == lo)
        need = kf - cnt_gt
        need_ref[...] = need
        bad_ref[...] = jnp.where((cnt_eq > need) & (lo > neg_inf_key), 1.0, 0.0)

    for n in range(1, seq // (2 * kc) + 1):
        pl.when(npair == n)(functools.partial(select_threshold, n))

    jl_ref[...] = jnp.full((1, qb), seq, I32)

    @pl.when(jnp.max(bad_ref[...]) > 0.0)
    def _():
        nbits = max(1, (seq - 1).bit_length())
        lo = lo_ref[...]
        need = need_ref[...]

        def count_ties_before(cand):
            def body(c, acc):
                ind = jnp.where((key_ref[c] == lo) & (c * kc + sub_pos < cand), 1.0, 0.0)
                return fold(ind, nacc, acc)
            acc = lax.fori_loop(0, 2 * npair, body, jnp.zeros((nacc, qb), F32))
            return jnp.sum(acc, axis=0, keepdims=True)

        def jbit_body(it, v):
            cand = v + lax.shift_left(jnp.int32(1), jnp.int32(nbits - 1) - it)
            return jnp.where(count_ties_before(cand) < need, cand, v)

        jl_ref[...] = lax.fori_loop(0, nbits, jbit_body, jnp.zeros((1, qb), I32))

    hg = 2 * qb
    m_ref[...] = jnp.full(m_ref.shape, NEG_BIG, F32)

    def max_chunk(c):
        kv = ckvn_ref[pl.ds(pl.multiple_of(c * kc, kc), kc), :]
        key = key_ref[c]
        pos = c * kc + sub_pos
        lo_c = lo_ref[...]
        tie = jnp.where(key == lo_c, jnp.where(pos <= jl_ref[...], 0.0, NEG_BIG), NEG_BIG)
        bias = jnp.where(pos <= t_row, jnp.where(key > lo_c, 0.0, tie), NEG_BIG)
        key_ref[c] = pltpu.bitcast(bias, I32)
        bias2 = jnp.concatenate([bias, bias], axis=1)
        for g in range(N_ATT_HEADS // 2):
            cols = slice(g * hg, (g + 1) * hg)
            s = _dot(kv, qlt_ref[:, cols]) + bias2
            m_ref[:, cols] = jnp.maximum(m_ref[:, cols], jnp.max(s, axis=0, keepdims=True))

    pair_loop(max_chunk)

    l_ref[...] = jnp.zeros(l_ref.shape, F32)
    acc_ref[...] = jnp.zeros(acc_ref.shape, F32)

    def att_chunk(c):
        kv = ckvn_ref[pl.ds(pl.multiple_of(c * kc, kc), kc), :]
        kvt = ckvt_ref[c]
        bias = pltpu.bitcast(key_ref[c], F32)
        bias2 = jnp.concatenate([bias, bias], axis=1)
        for g in range(N_ATT_HEADS // 2):
            cols = slice(g * hg, (g + 1) * hg)
            p = jnp.exp2(_dot(kv, qlt_ref[:, cols]) + bias2 - m_ref[:, cols])
            l_ref[:, cols] = l_ref[:, cols] + jnp.sum(p, axis=0, keepdims=True)
            acc_ref[:, cols] = acc_ref[:, cols] + _dot(kvt, p.astype(BF16))

    pair_loop(att_chunk)

    o_t = (acc_ref[...] * (1.0 / l_ref[...])).astype(BF16)
    for h in range(N_ATT_HEADS):
        ot_h = _dot(wuvt_ref[h], o_t[:, h * qb:(h + 1) * qb])
        o_ref[:, h * ATT_HEAD_DIM:(h + 1) * ATT_HEAD_DIM] = jnp.transpose(ot_h).astype(o_ref.dtype)


def _dsa_attention(qidx, qlt, kw, kidx, ckvn, ckvt, w_uvt, bsz, seq):
    t = kw.shape[0]
    qb = Q_BLOCK
    kc = min(DSA_KEY_CHUNK, seq)
    nb = seq // qb
    k_sel = min(TOPK_MAX, seq // 4)
    assert seq % (2 * kc) == 0
    kern = functools.partial(_dsa_attn_kernel, kc=kc, k_sel=k_sel, seq=seq)
    return pl.pallas_call(
        kern,
        out_shape=jax.ShapeDtypeStruct((t, ATT_WIDTH), BF16),
        grid=(bsz, nb),
        in_specs=[
            pl.BlockSpec((None, N_IDX_HEADS, qb, LANES), lambda b, i: (b, 0, i, 0)),
            pl.BlockSpec((None, None, KV_RANK, N_ATT_HEADS * qb), lambda b, i: (b, i, 0, 0)),
            pl.BlockSpec((qb, LANES), lambda b, i: (b * nb + i, 0)),
            pl.BlockSpec((seq, LANES), lambda b, i: (b, 0)),
            pl.BlockSpec((seq, KV_RANK), lambda b, i: (b, 0)),
            pl.BlockSpec((None, seq // kc, KV_RANK, kc), lambda b, i: (b, 0, 0, 0)),
            pl.BlockSpec((N_ATT_HEADS, ATT_HEAD_DIM, KV_RANK), lambda b, i: (0, 0, 0)),
        ],
        out_specs=pl.BlockSpec((qb, ATT_WIDTH), lambda b, i: (b * nb + i, 0)),
        scratch_shapes=[
            pltpu.VMEM((seq // kc, kc, qb), I32),
            pltpu.VMEM((seq // kc, kc, qb), I16),
            pltpu.VMEM((1, N_ATT_HEADS * qb), F32),
            pltpu.VMEM((1, N_ATT_HEADS * qb), F32),
            pltpu.VMEM((KV_RANK, N_ATT_HEADS * qb), F32),
            pltpu.VMEM((1, qb), I32),
            pltpu.VMEM((1, qb), I32),
            pltpu.VMEM((1, qb), F32),
            pltpu.VMEM((1, qb), F32),
        ],
        compiler_params=_cparams(("arbitrary", "arbitrary")),
        name="dsa_attention",
    )(qidx, qlt, kw, kidx, ckvn, ckvt, w_uvt)


def _log_sigmoid(x):
    return jnp.minimum(x, 0.0) - jnp.log(1.0 + jnp.exp(-jnp.abs(x)))


def _mlstm_kernel(q_ref, k_ref, v_ref, og_ref, kw_ref, gb_ref, gn_ref, out_ref,
                  c_ref, n_ref, m_ref, *, chunk):
    lc = chunk
    dk, dv = MLSTM_QK_DIM, MLSTM_V_DIM

    @pl.when(pl.program_id(1) == 0)
    def _():
        c_ref[...] = jnp.zeros(c_ref.shape, F32)
        n_ref[...] = jnp.zeros(n_ref.shape, F32)
        m_ref[...] = jnp.zeros(m_ref.shape, F32)

    kwb = kw_ref[...] + gb_ref[...]
    kwt = jnp.transpose(kwb)
    row = lax.broadcasted_iota(I32, (lc, lc), 0)
    col = lax.broadcasted_iota(I32, (lc, lc), 1)
    causal = col <= row
    tri = jnp.where(causal, 1.0, 0.0)
    tri_t = jnp.where(row <= col, 1.0, 0.0)
    cum_col = _dot_f32(tri, _log_sigmoid(kwb))
    cum_row = _dot_f32(_log_sigmoid(kwt), tri_t)

    for h in range(N_MLSTM_HEADS):
        i_col = kwb[:, KW_MI + h:KW_MI + h + 1]
        i_row = kwt[KW_MI + h:KW_MI + h + 1, :]
        b_col = cum_col[:, KW_MF + h:KW_MF + h + 1]
        b_row = cum_row[KW_MF + h:KW_MF + h + 1, :]
        m_prev = m_ref[h]

        dmat = jnp.where(causal, b_col - b_row + i_row, -jnp.inf)
        m_inter = b_col + m_prev
        m_j = jnp.maximum(m_inter, jnp.max(dmat, axis=1, keepdims=True))
        w_inter = jnp.exp(m_inter - m_j)

        qh = q_ref[:, h * dk:(h + 1) * dk].astype(F32) * (dk ** -0.5)
        kh = k_ref[:, h * dk:(h + 1) * dk].astype(F32)
        vb = v_ref[:, h * dv:(h + 1) * dv]
        qb = qh.astype(BF16)
        sc = _dot_nt(qb, kh.astype(BF16)) * jnp.exp(dmat - m_j)
        c_old = c_ref[h]
        n_old = n_ref[h]
        num = w_inter * _dot(qb, c_old.astype(BF16)) + _dot(sc.astype(BF16), vb)
        den = (w_inter * jnp.sum(qh * n_old, axis=1, keepdims=True)
               + jnp.sum(sc, axis=1, keepdims=True))
        hval = num / jnp.maximum(jnp.abs(den), jnp.exp(-m_j))

        hn = _rms(hval) * gn_ref[:, h * dv:(h + 1) * dv]
        gate = jax.nn.sigmoid(og_ref[:, h * dv:(h + 1) * dv].astype(F32))
        out_ref[:, h * dv:(h + 1) * dv] = (hn * gate).astype(out_ref.dtype)

        b_last = b_col[lc - 1:lc, :]
        g_col = b_last - b_col + i_col
        m_new = jnp.maximum(b_last + m_prev, jnp.max(g_col, axis=0, keepdims=True))
        w_old = jnp.exp(b_last + m_prev - m_new)
        kwgt = kh * jnp.exp(g_col - m_new)
        c_ref[h] = w_old * c_old + _dot(jnp.transpose(kwgt).astype(BF16), vb)
        n_ref[h] = w_old * n_old + jnp.sum(kwgt, axis=0, keepdims=True)
        m_ref[h] = m_new


def _mlstm(z, kw, gate_bias, norm_g, bsz, seq):
    t = z.shape[0]
    lc = min(256, seq)
    nc = seq // lc
    qk_w = N_MLSTM_HEADS * MLSTM_QK_DIM
    kern = functools.partial(_mlstm_kernel, chunk=lc)
    row = lambda b, c: b * nc + c
    return pl.pallas_call(
        kern,
        out_shape=jax.ShapeDtypeStruct((t, MLSTM_WIDTH), BF16),
        grid=(bsz, nc),
        in_specs=[
            pl.BlockSpec((lc, qk_w), lambda b, c: (row(b, c), Z_MQ // qk_w)),
            pl.BlockSpec((lc, qk_w), lambda b, c: (row(b, c), Z_MK // qk_w)),
            pl.BlockSpec((lc, MLSTM_WIDTH), lambda b, c: (row(b, c), Z_MV // MLSTM_WIDTH)),
            pl.BlockSpec((lc, MLSTM_WIDTH), lambda b, c: (row(b, c), Z_MO // MLSTM_WIDTH)),
            pl.BlockSpec((lc, LANES), lambda b, c: (row(b, c), 0)),
            pl.BlockSpec((1, LANES), lambda b, c: (0, 0)),
            pl.BlockSpec((1, MLSTM_WIDTH), lambda b, c: (0, 0)),
        ],
        out_specs=pl.BlockSpec((lc, MLSTM_WIDTH), lambda b, c: (row(b, c), 0)),
        scratch_shapes=[
            pltpu.VMEM((N_MLSTM_HEADS, MLSTM_QK_DIM, MLSTM_V_DIM), F32),
            pltpu.VMEM((N_MLSTM_HEADS, 1, MLSTM_QK_DIM), F32),
            pltpu.VMEM((N_MLSTM_HEADS, 1, 1), F32),
        ],
        compiler_params=_cparams(("arbitrary", "arbitrary")),
        name="mlstm",
    )(z, z, z, z, kw, gate_bias, norm_g)


def _outproj_router_kernel(att_ref, hm_ref, x_ref, g1_ref, gn_ref, sc_ref, sh_ref, wo_ref, rwa_ref, rb_ref,
                           x1_ref, h2_ref, ri_ref, rf_ref, cnt_ref, carry_ref, *, tm):
    @pl.when(pl.program_id(0) == 0)
    def _():
        carry_ref[...] = jnp.zeros(carry_ref.shape, F32)

    y = _dot(att_ref[...], wo_ref[:ATT_WIDTH, :]) + _dot(hm_ref[...], wo_ref[ATT_WIDTH:, :])
    x1 = x_ref[...] + g1_ref[...] * y
    x1_ref[...] = x1
    h2 = _rms(x1) * gn_ref[...] * (1.0 + sc_ref[...]) + sh_ref[...]
    h2_ref[...] = h2

    h2_hi = h2.astype(BF16)
    h2_lo = (h2 - h2_hi.astype(F32)).astype(BF16)
    both = _dot(h2_hi, rwa_ref[...])
    logits = both[:, :LANES] + both[:, LANES:] + _dot(h2_lo, rwa_ref[:, :LANES])
    aff = jax.nn.sigmoid(jnp.transpose(logits)[:N_EXPERTS, :])
    sel = aff + rb_ref[...]

    epg = EXPERTS_PER_GROUP
    riota = lax.broadcasted_iota(I32, (epg, tm), 0)
    best = None
    for g in range(N_EXPERT_GROUPS):
        v = sel[g * epg:(g + 1) * epg]
        a = aff[g * epg:(g + 1) * epg]
        m1 = jnp.max(v, axis=0, keepdims=True)
        i1 = jnp.min(jnp.where(v == m1, riota, epg), axis=0, keepdims=True)
        v2 = jnp.where(riota == i1, -jnp.inf, v)
        m2 = jnp.max(v2, axis=0, keepdims=True)
        i2 = jnp.min(jnp.where(v2 == m2, riota, epg), axis=0, keepdims=True)
        a1 = jnp.sum(jnp.where(riota == i1, a, 0.0), axis=0, keepdims=True)
        a2 = jnp.sum(jnp.where(riota == i2, a, 0.0), axis=0, keepdims=True)
        cur = (m1 + m2, i1 + g * epg, i2 + g * epg, a1, a2)
        if best is None:
            best = cur
        else:
            better = cur[0] > best[0]
            best = tuple(jnp.where(better, c_, b_) for c_, b_ in zip(cur, best))
    _, e1, e2, a1, a2 = best
    asum = a1 + a2

    eiota = lax.broadcasted_iota(I32, (N_EXPERTS, tm), 0)
    hit1 = eiota == e1
    hit2 = eiota == e2
    onehot = jnp.where(hit1, 1.0, jnp.where(hit2, 1.0, 0.0))
    srow = lax.broadcasted_iota(I32, (tm, tm), 0)
    scol = lax.broadcasted_iota(I32, (tm, tm), 1)
    before = jnp.where(srow < scol, 1.0, 0.0).astype(BF16)
    rank = _dot(onehot.astype(BF16), before) + carry_ref[...]
    r1 = jnp.sum(jnp.where(hit1, rank, 0.0), axis=0, keepdims=True)
    r2 = jnp.sum(jnp.where(hit2, rank, 0.0), axis=0, keepdims=True)
    carry = carry_ref[...] + jnp.sum(onehot, axis=1, keepdims=True)
    carry_ref[...] = carry
    cnt_ref[...] = jnp.broadcast_to(carry, cnt_ref.shape)

    zi = jnp.zeros((1, tm), I32)
    ri_ref[...] = jnp.concatenate(
        [e1, e2, r1.astype(I32), r2.astype(I32), zi, zi, zi, zi], axis=0)
    zf = jnp.zeros((1, tm), F32)
    rf_ref[...] = jnp.concatenate([a1 / asum, a2 / asum, zf, zf, zf, zf, zf, zf], axis=0)


def _outproj_router(att, hm, x2d, mod_l, gain, w_out_b, router_w, router_b, seq):
    t, d = x2d.shape
    tm = min(256, seq)
    per_b = seq // tm
    nt = t // tm
    kern = functools.partial(_outproj_router_kernel, tm=tm)
    rw = jnp.pad(router_w, ((0, 0), (0, LANES - N_EXPERTS)))
    rw_hi = rw.astype(BF16)
    rw_lo = (rw - rw_hi.astype(F32)).astype(BF16)
    rw_split = jnp.concatenate([rw_hi, rw_lo], axis=1)
    mod_spec = lambda which: pl.BlockSpec((None, None, 1, d), lambda i: (i // per_b, which, 0, 0))
    return pl.pallas_call(
        kern,
        out_shape=(
            jax.ShapeDtypeStruct((t, d), F32),
            jax.ShapeDtypeStruct((t, d), F32),
            jax.ShapeDtypeStruct((nt, SUBLANES, tm), I32),
            jax.ShapeDtypeStruct((nt, SUBLANES, tm), F32),
            jax.ShapeDtypeStruct((N_EXPERTS, LANES), F32),
        ),
        grid=(nt,),
        in_specs=[
            pl.BlockSpec((tm, ATT_WIDTH), lambda i: (i, 0)),
            pl.BlockSpec((tm, MLSTM_WIDTH), lambda i: (i, 0)),
            pl.BlockSpec((tm, d), lambda i: (i, 0)),
            mod_spec(2),
            pl.BlockSpec((1, d), lambda i: (0, 0)),
            mod_spec(4),
            mod_spec(3),
            pl.BlockSpec((ATT_WIDTH + MLSTM_WIDTH, d), lambda i: (0, 0)),
            pl.BlockSpec((d, 2 * LANES), lambda i: (0, 0)),
            pl.BlockSpec((N_EXPERTS, 1), lambda i: (0, 0)),
        ],
        out_specs=(
            pl.BlockSpec((tm, d), lambda i: (i, 0)),
            pl.BlockSpec((tm, d), lambda i: (i, 0)),
            pl.BlockSpec((None, SUBLANES, tm), lambda i: (i, 0, 0)),
            pl.BlockSpec((None, SUBLANES, tm), lambda i: (i, 0, 0)),
            pl.BlockSpec((N_EXPERTS, LANES), lambda i: (0, 0)),
        ),
        scratch_shapes=[pltpu.VMEM((N_EXPERTS, 1), F32)],
        compiler_params=_cparams(("arbitrary",)),
        name="outproj_router",
    )(att, hm, x2d, mod_l, gain.reshape(1, d), mod_l, mod_l, w_out_b, rw_split, router_b.reshape(N_EXPERTS, 1))


def _dispatch_kernel(pos_ref, src_ref, dst_ref, sem, *, tc):
    i = pl.program_id(0)

    def row_copy(r, slot):
        p = pos_ref[2 * (i * tc + r) + slot]
        return pltpu.make_async_copy(src_ref.at[pl.ds(r, 1)], dst_ref.at[pl.ds(p, 1)], sem)

    def start_body(r, carry):
        row_copy(r, 0).start()
        row_copy(r, 1).start()
        return carry

    lax.fori_loop(0, tc, start_body, 0, unroll=8)

    def wait_body(r, carry):
        row_copy(r, 0).wait()
        row_copy(r, 1).wait()
        return carry

    lax.fori_loop(0, tc, wait_body, 0, unroll=8)


def _dispatch(pos, h2, seq):
    t, d = h2.shape
    tc = min(256, seq)
    kern = functools.partial(_dispatch_kernel, tc=tc)
    return pl.pallas_call(
        kern,
        out_shape=jax.ShapeDtypeStruct((2 * t, d), h2.dtype),
        grid_spec=pltpu.PrefetchScalarGridSpec(
            num_scalar_prefetch=1,
            grid=(t // tc,),
            in_specs=[pl.BlockSpec((tc, d), lambda i, pos: (i, 0))],
            out_specs=pl.BlockSpec(memory_space=pl.ANY),
            scratch_shapes=[pltpu.SemaphoreType.DMA],
        ),
        compiler_params=pltpu.CompilerParams(dimension_semantics=("arbitrary",), has_side_effects=True,
                                             vmem_limit_bytes=VMEM_LIMIT),
        name="moe_dispatch",
    )(pos, h2)


def _gmm_kernel(vt_ref, ve_ref, vfirst_ref, vnew_ref, vstart_ref, vend_ref, xs_ref, wgu_ref, wd_ref, ys_ref,
                wgu_b, wd_b, *, tm):
    v = pl.program_id(0)
    start = vstart_ref[v]
    end = vend_ref[v]

    @pl.when(vnew_ref[v] == 1)
    def _():
        wgu_b[...] = wgu_ref[...].astype(BF16)
        wd_b[...] = wd_ref[...].astype(BF16)

    @pl.when(end > start)
    def _():
        x = xs_ref[...].astype(BF16)
        gu = _dot(x, wgu_b[...])
        gt = gu[:, :D_FF_EXPERT]
        up = gu[:, D_FF_EXPERT:]
        act = (gt * jax.nn.sigmoid(gt)) * up
        y = _dot(act.astype(BF16), wd_b[...])
        rows = vt_ref[v] * tm + lax.broadcasted_iota(I32, (tm, 1), 0)
        mine = (rows >= start) & (rows < end)

        @pl.when(vfirst_ref[v] == 1)
        def _():
            ys_ref[...] = jnp.where(mine, y, 0.0)

        @pl.when(vfirst_ref[v] == 0)
        def _():
            ys_ref[...] = jnp.where(mine, y, ys_ref[...])


def _gmm(sched, xs, w_gate_up, w_down, layer, tm):
    rows, d = xs.shape
    n_visits = sched[0].shape[0]
    kern = functools.partial(_gmm_kernel, tm=tm)
    return pl.pallas_call(
        kern,
        out_shape=jax.ShapeDtypeStruct((rows, d), F32),
        grid_spec=pltpu.PrefetchScalarGridSpec(
            num_scalar_prefetch=6,
            grid=(n_visits,),
            in_specs=[
                pl.BlockSpec((tm, d), lambda v, vt, ve, *_: (vt[v], 0)),
                pl.BlockSpec((None, None, d, 2 * D_FF_EXPERT), lambda v, vt, ve, *_: (layer, ve[v], 0, 0)),
                pl.BlockSpec((None, None, D_FF_EXPERT, d), lambda v, vt, ve, *_: (layer, ve[v], 0, 0)),
            ],
            out_specs=pl.BlockSpec((tm, d), lambda v, vt, ve, *_: (vt[v], 0)),
            scratch_shapes=[pltpu.VMEM((d, 2 * D_FF_EXPERT), BF16), pltpu.VMEM((D_FF_EXPERT, d), BF16)],
        ),
        compiler_params=_cparams(("arbitrary",)),
        name="moe_experts",
    )(*sched, xs, w_gate_up, w_down)


def _visit_schedule(counts, n_rows, tm):
    ends = jnp.cumsum(counts)
    starts = ends - counts
    n_tiles = n_rows // tm
    n_visits = n_tiles + N_EXPERTS - 1
    first_tile = starts // tm
    last_tile = jnp.maximum(ends - 1, 0) // tm
    nv = jnp.where(counts > 0, last_tile - first_tile + 1, 0)
    vend_cum = jnp.cumsum(nv)
    voff = vend_cum - nv
    total = vend_cum[-1]
    v = jnp.arange(n_visits, dtype=I32)
    live = v < total
    vq = jnp.minimum(v, total - 1)
    e = jnp.sum((vq[:, None] >= vend_cum[None, :]).astype(I32), axis=1)
    e = jnp.minimum(e, N_EXPERTS - 1)
    owner = e[:, None] == jnp.arange(N_EXPERTS, dtype=I32)[None, :]

    def per_visit(table):
        return jnp.sum(jnp.where(owner, table[None, :], 0), axis=1)

    tile = jnp.where(live, per_visit(first_tile) + (v - per_visit(voff)), n_tiles - 1).astype(I32)
    prev_tile = jnp.concatenate([jnp.full((1,), -1, I32), tile[:-1]])
    first = jnp.where(live & (tile != prev_tile), 1, 0).astype(I32)
    prev_e = jnp.concatenate([jnp.full((1,), -1, I32), e[:-1]])
    new_e = jnp.where(live & (e != prev_e), 1, 0).astype(I32)
    vstart = jnp.where(live, per_visit(starts), 0).astype(I32)
    vend = jnp.where(live, per_visit(ends), 0).astype(I32)
    return (tile, e, first, new_e, vstart, vend), starts


def _combine_kernel(pos_ref, ys_ref, x1_ref, ga_ref, gb_ref, g2_ref, fg_ref, o_ref, buf_ref, sems, *, tc, final):
    i = pl.program_id(0)
    slot = i % 2

    def row_copy(tile, dst_slot, r, k):
        p = pos_ref[2 * (tile * tc + r) + k]
        return pltpu.make_async_copy(ys_ref.at[pl.ds(p, 1)], buf_ref.at[dst_slot, k, pl.ds(r, 1)], sems.at[dst_slot])

    def start_tile(tile, dst_slot):
        def body(r, carry):
            row_copy(tile, dst_slot, r, 0).start()
            row_copy(tile, dst_slot, r, 1).start()
            return carry
        lax.fori_loop(0, tc, body, 0, unroll=8)

    def wait_tile(tile, dst_slot):
        def body(r, carry):
            row_copy(tile, dst_slot, r, 0).wait()
            row_copy(tile, dst_slot, r, 1).wait()
            return carry
        lax.fori_loop(0, tc, body, 0, unroll=8)

    @pl.when(i == 0)
    def _():
        start_tile(0, 0)

    @pl.when(i + 1 < pl.num_programs(0))
    def _():
        start_tile(i + 1, 1 - slot)

    wait_tile(i, slot)

    moe = buf_ref[slot, 0] * ga_ref[...] + buf_ref[slot, 1] * gb_ref[...]
    x2 = x1_ref[...] + g2_ref[...] * moe
    if final:
        x2 = _rms(x2) * fg_ref[...]
    o_ref[...] = x2


def _combine(pos, ys, x1, gate_a, gate_b, mod_l, final_g, seq, final):
    t, d = x1.shape
    tc = min(256, seq)
    per_b = seq // tc
    kern = functools.partial(_combine_kernel, tc=tc, final=final)
    return pl.pallas_call(
        kern,
        out_shape=jax.ShapeDtypeStruct((t, d), F32),
        grid_spec=pltpu.PrefetchScalarGridSpec(
            num_scalar_prefetch=1,
            grid=(t // tc,),
            in_specs=[
                pl.BlockSpec(memory_space=pl.ANY),
                pl.BlockSpec((tc, d), lambda i, pos: (i, 0)),
                pl.BlockSpec((tc, 1), lambda i, pos: (i, 0)),
                pl.BlockSpec((tc, 1), lambda i, pos: (i, 0)),
                pl.BlockSpec((None, None, 1, d), lambda i, pos: (i // per_b, 5, 0, 0)),
                pl.BlockSpec((1, d), lambda i, pos: (0, 0)),
            ],
            out_specs=pl.BlockSpec((tc, d), lambda i, pos: (i, 0)),
            scratch_shapes=[pltpu.VMEM((2, 2, tc, d), F32), pltpu.SemaphoreType.DMA((2,))],
        ),
        compiler_params=_cparams(("arbitrary",)),
        name="moe_combine",
    )(pos, ys, x1, gate_a, gate_b, mod_l, final_g.reshape(1, d))


def _rearrange_w_in(w):
    o_kidx = Q_RANK + KV_RANK
    o_mq = o_kidx + IDX_DIM + N_IDX_HEADS
    o_mi = o_mq + 2 * N_MLSTM_HEADS * MLSTM_QK_DIM + 2 * MLSTM_WIDTH
    gates = w[:, o_mi:o_mi + 2 * N_MLSTM_HEADS]
    small = jnp.concatenate([w[:, o_kidx:o_mq], gates], axis=1)
    small = jnp.pad(small, ((0, 0), (0, Z_MQ - Z_KW - small.shape[1])))
    return jnp.concatenate([w[:, :o_kidx], small, w[:, o_mq:o_mi]], axis=1).astype(BF16)


def kernel(x, c, ada_w, ada_b, mix_norm_g, w_in, cq_norm_g, ckv_norm_g, w_uq, w_uk, w_uv, w_qidx,
           mlstm_i_b, mlstm_f_b, mlstm_norm_g, w_out, ffn_norm_g, router_w, router_b, w_gate_up,
           w_down, final_norm_g):
    bsz, seq, d = x.shape
    depth = ada_w.shape[0]
    t = bsz * seq
    assert seq % Q_BLOCK == 0 and seq % min(512, seq) == 0

    mod = _modulation(c, ada_w, ada_b).reshape(depth, bsz, 6, 1, d)
    gmm_tm = 256

    x2d = x.reshape(t, d)
    for l in range(depth):
        mod_l = mod[l]
        z, kw = _in_projection(x2d, mix_norm_g[l], mod_l, _rearrange_w_in(w_in[l]), seq)
        w_ukh = jnp.transpose(w_uk[l], (1, 0, 2)).astype(BF16)
        w_uvt = jnp.transpose(w_uv[l], (1, 2, 0)).astype(BF16)
        w_qidx_p = jnp.pad(w_qidx[l].reshape(Q_RANK, N_IDX_HEADS, IDX_DIM),
                           ((0, 0), (0, 0), (0, LANES - IDX_DIM))).reshape(Q_RANK, N_IDX_HEADS * LANES).astype(BF16)
        qlt, qidx, ckvn, ckvt, kidx = _dsa_prep(z, kw, cq_norm_g[l].reshape(1, Q_RANK),
                                                ckv_norm_g[l].reshape(1, KV_RANK),
                                                w_uq[l].astype(BF16), w_ukh, w_qidx_p, bsz, seq)
        att = _dsa_attention(qidx, qlt, kw, kidx, ckvn, ckvt, w_uvt, bsz, seq)
        gate_bias = jnp.concatenate([jnp.zeros((KW_MI,), F32), mlstm_i_b[l], mlstm_f_b[l],
                                     jnp.zeros((LANES - KW_MF - N_MLSTM_HEADS,), F32)]).reshape(1, LANES)
        hm = _mlstm(z, kw, gate_bias, mlstm_norm_g[l].reshape(1, MLSTM_WIDTH), bsz, seq)
        x1, h2, ri, rf, cnt = _outproj_router(att, hm, x2d, mod_l, ffn_norm_g[l], w_out[l].astype(BF16),
                                              router_w, router_b, seq)
        counts = cnt[:, 0].astype(I32)
        sched, starts = _visit_schedule(counts, 2 * t, gmm_tm)
        e1 = ri[:, 0, :].reshape(t)
        e2 = ri[:, 1, :].reshape(t)
        pos = jnp.stack([starts[e1] + ri[:, 2, :].reshape(t), starts[e2] + ri[:, 3, :].reshape(t)],
                        axis=-1).reshape(2 * t).astype(I32)
        xs = _dispatch(pos, h2, seq)
        ys = _gmm(sched, xs, w_gate_up, w_down, l, gmm_tm)
        x2d = _combine(pos, ys, x1, rf[:, 0, :].reshape(t, 1), rf[:, 1, :].reshape(t, 1), mod_l,
                       final_norm_g, seq, final=(l == depth - 1))
    return x2d.reshape(bsz, seq, d)
```
